```python
import jax
import jax.numpy as jnp
from jax import lax
import numpy as np

D_MODEL = 2048
BATCH = 2
SEQ = 4096
DEPTH = 2
DEC_BATCH = 128
DEC_SEQ = 4
PAST_LEN = 2048
PAGE_SIZE = 128

SB_HEADS = 8
SB_HEAD_DIM = 128
SB_WIDTH = SB_HEADS * SB_HEAD_DIM
SB_QBLOCK = 128
SB_BIAS_INIT = -6.0
POOL_WINDOWS = (2, 4, 8, 16)
POOL_GROUPS = len(POOL_WINDOWS)
POOL_WIDTH = D_MODEL - SB_WIDTH
POOL_GROUP_DIM = POOL_WIDTH // POOL_GROUPS
POOL_HIST = max(POOL_WINDOWS) - 1
CONV_CH = D_MODEL
CONV_WIDTH = 31
CONV_HIST = CONV_WIDTH - 1
D_FF = 5632
N_EXPERTS = 8
TOP_K = 2
D_FF_EXPERT = 5632
N_EVEN = (DEPTH + 1) // 2
N_ODD = DEPTH // 2
RMS_EPS = 1e-6
LN_EPS = 1e-5

kernel_name = 'stickbreak_pool_conformer_moe_step'


def rmsnorm(x, g):
    xf = x.astype(jnp.float32)
    y = xf * lax.rsqrt(jnp.mean(xf * xf, axis=-1, keepdims=True) + RMS_EPS)
    return (y * g.astype(jnp.float32)).astype(x.dtype)


def sb_weights(z, valid):
    log_keep = jnp.where(valid, jax.nn.log_sigmoid(-z), 0.0)
    between = lax.cumsum(log_keep, axis=z.ndim - 1, reverse=True) - log_keep
    return jnp.where(valid, jnp.exp(jax.nn.log_sigmoid(z) + between), 0.0)


def sb_attend_prompt(q, k, v, bias):
    b, s, h, dh = q.shape
    nb = s // SB_QBLOCK
    qb = jnp.swapaxes(q.reshape(b, nb, SB_QBLOCK, h, dh), 0, 1)
    kpos = jnp.arange(s)
    scale = SB_HEAD_DIM ** -0.5
    bias_f = bias.astype(jnp.float32)[None, :, None, None]

    def block(args):
        qblk, bi = args
        qpos = bi * SB_QBLOCK + jnp.arange(SB_QBLOCK)
        z = jnp.einsum('bqhd,bkhd->bhqk', qblk, k).astype(jnp.float32) * scale + bias_f
        a = sb_weights(z, kpos[None, :] < qpos[:, None])
        return jnp.einsum('bhqk,bkhd->bqhd', a.astype(v.dtype), v)

    out = lax.map(block, (qb, jnp.arange(nb)))
    return jnp.swapaxes(out, 0, 1).reshape(b, s, h, dh)


def sb_attend_cached(q, k, v, k_past, v_past, bias):
    t = q.shape[1]
    p = k_past.shape[1]
    scale = SB_HEAD_DIM ** -0.5
    z = jnp.concatenate([jnp.einsum('bqhd,bkhd->bhqk', q, k_past),
                         jnp.einsum('bqhd,bkhd->bhqk', q, k)], axis=-1).astype(jnp.float32) * scale
    z = z + bias.astype(jnp.float32)[None, :, None, None]
    kpos = jnp.arange(p + t)
    qpos = p + jnp.arange(t)
    a = sb_weights(z, kpos[None, :] < qpos[:, None]).astype(v.dtype)
    return (jnp.einsum('bhqk,bkhd->bqhd', a[..., :p], v_past)
            + jnp.einsum('bhqk,bkhd->bqhd', a[..., p:], v))


def pool_mixer(u, hist, pos0, w_group, scale):
    n, l, _ = u.shape
    full = jnp.concatenate([hist, u], axis=1)
    c = jnp.cumsum(jnp.pad(full.astype(jnp.float32), ((0, 0), (1, 0), (0, 0))), axis=1)
    m = POOL_HIST + 1
    pos = pos0 + jnp.arange(l)
    means = []
    for g, w in enumerate(POOL_WINDOWS):
        ch = slice(g * POOL_GROUP_DIM, (g + 1) * POOL_GROUP_DIM)
        wsum = c[:, m:m + l, ch] - c[:, m - w:m - w + l, ch]
        count = jnp.minimum(pos + 1, w).astype(jnp.float32)
        means.append(wsum / count[None, :, None])
    mean = jnp.stack(means, axis=2)
    diff = (mean - u.astype(jnp.float32).reshape(n, l, POOL_GROUPS, POOL_GROUP_DIM)).astype(u.dtype)
    y = jnp.einsum('nlgc,gcd->nlgd', diff, w_group).reshape(n, l, POOL_WIDTH) * scale
    return y, full[:, -POOL_HIST:]


def conformer_conv(xn, hist, w_in, dw, dw_bias, ln_g, ln_b, w_out):
    a, gate = jnp.split(xn @ w_in, 2, axis=-1)
    u = a * jax.nn.sigmoid(gate)
    full = jnp.concatenate([hist, u], axis=1)
    h = lax.conv_general_dilated(full, dw[:, None, :], window_strides=(1,), padding='VALID',
                                 dimension_numbers=('NWC', 'WIO', 'NWC'),
                                 feature_group_count=CONV_CH) + dw_bias
    hf = h.astype(jnp.float32)
    mu = jnp.mean(hf, axis=-1, keepdims=True)
    var = jnp.mean(jnp.square(hf - mu), axis=-1, keepdims=True)
    hn = (hf - mu) * lax.rsqrt(var + LN_EPS) * ln_g.astype(jnp.float32) + ln_b.astype(jnp.float32)
    out = jax.nn.silu(hn).astype(xn.dtype) @ w_out
    return out, full[:, -CONV_HIST:]


def swiglu(x, wg, wu, wd):
    return (jax.nn.silu(x @ wg) * (x @ wu)) @ wd


def moe_swiglu(x, w_router, wg, wu, wd):
    n, l, d = x.shape
    xt = x.reshape(n * l, d)
    logits = (xt @ w_router).astype(jnp.float32)
    top_val, top_idx = lax.top_k(logits, TOP_K)
    gates = jax.nn.softmax(top_val, axis=-1)
    combine = jnp.sum(jax.nn.one_hot(top_idx, N_EXPERTS, dtype=jnp.float32) * gates[..., None], axis=1)
    combine = combine.astype(x.dtype)
    out = jnp.zeros_like(xt)
    for e in range(N_EXPERTS):
        out = out + combine[:, e:e + 1] * swiglu(xt, wg[e], wu[e], wd[e])
    return out.reshape(n, l, d)


def run_trunk(x, pos0, pool_hist, conv_hist, attend, p):
    n, l, _ = x.shape
    new_k, new_v, new_pool, new_conv = [], [], [], []
    for li in range(DEPTH):
        xn = rmsnorm(x, p['norm_mix'][li])
        if li % 2 == 0:
            e = li // 2
            proj = xn @ p['w_in_ab'][e]
            q, k, v, u = jnp.split(proj, [SB_WIDTH, 2 * SB_WIDTH, 3 * SB_WIDTH], axis=-1)
            q, k, v = (t.reshape(n, l, SB_HEADS, SB_HEAD_DIM) for t in (q, k, v))
            att = attend(e, q, k, v, p['sb_bias'][e]).reshape(n, l, SB_WIDTH)
            pooled, pool_rows = pool_mixer(u, pool_hist[e], pos0, p['w_pool'][e], p['pool_scale'][e])
            x = x + jnp.concatenate([att, pooled], axis=-1) @ p['w_out_ab'][e]
            x = x + swiglu(rmsnorm(x, p['norm_ffn'][li]), p['w_ffn_gate'][e],
                           p['w_ffn_up'][e], p['w_ffn_down'][e])
            new_k.append(k)
            new_v.append(v)
            new_pool.append(pool_rows)
        else:
            o = li // 2
            c_out, conv_rows = conformer_conv(xn, conv_hist[o], p['w_in_conv'][o], p['conv_dw'][o],
                                              p['conv_dw_bias'][o], p['conv_ln_gain'][o],
                                              p['conv_ln_bias'][o], p['w_out_conv'][o])
            x = x + c_out
            x = x + moe_swiglu(rmsnorm(x, p['norm_ffn'][li]), p['w_router'][o], p['w_exp_gate'][o],
                               p['w_exp_up'][o], p['w_exp_down'][o])
            new_conv.append(conv_rows)
    y = rmsnorm(x, p['norm_out'])
    return y, jnp.stack(new_k), jnp.stack(new_v), jnp.stack(new_pool), jnp.stack(new_conv)


def setup_inputs(seed: int = 0) -> dict:
    key = jax.random.key(seed)
    ks = iter(jax.random.split(key, 40))

    def nrm(shape, scale):
        return jax.random.normal(next(ks), shape, jnp.float32) * scale

    d = D_MODEL
    n_pages = PAST_LEN // PAGE_SIZE
    n_used = DEC_BATCH * n_pages
    n_pool = n_used + n_used // 4
    x_prompt = nrm((BATCH, SEQ, d), 1.0)
    x_sample = nrm((DEC_BATCH, DEC_SEQ, d), 1.0)
    cache_k = nrm((N_EVEN, n_pool, PAGE_SIZE, SB_HEADS, SB_HEAD_DIM), 1.0)
    cache_v = nrm((N_EVEN, n_pool, PAGE_SIZE, SB_HEADS, SB_HEAD_DIM), 1.0)
    state_pool = nrm((N_EVEN, DEC_BATCH, POOL_HIST, POOL_WIDTH), 1.0)
    state_conv = nrm((N_ODD, DEC_BATCH, CONV_HIST, CONV_CH), 0.5)
    page_table = jax.random.permutation(next(ks), n_pool)[:n_used].reshape(DEC_BATCH, n_pages).astype(jnp.int32)
    return {
        'x_prompt': x_prompt,
        'x_sample': x_sample,
        'cache_k': cache_k,
        'cache_v': cache_v,
        'state_pool': state_pool,
        'state_conv': state_conv,
        'page_table': page_table,
        'norm_mix': 1.0 + nrm((DEPTH, d), 0.1),
        'norm_ffn': 1.0 + nrm((DEPTH, d), 0.1),
        'norm_out': 1.0 + nrm((d,), 0.1),
        'w_in_ab': nrm((N_EVEN, d, 3 * SB_WIDTH + POOL_WIDTH), d ** -0.5),
        'sb_bias': SB_BIAS_INIT + nrm((N_EVEN, SB_HEADS), 0.5),
        'w_pool': nrm((N_EVEN, POOL_GROUPS, POOL_GROUP_DIM, POOL_GROUP_DIM), POOL_GROUP_DIM ** -0.5),
        'pool_scale': 1.0 + nrm((N_EVEN, POOL_WIDTH), 0.1),
        'w_out_ab': nrm((N_EVEN, SB_WIDTH + POOL_WIDTH, d), (SB_WIDTH + POOL_WIDTH) ** -0.5),
        'w_ffn_gate': nrm((N_EVEN, d, D_FF), d ** -0.5),
        'w_ffn_up': nrm((N_EVEN, d, D_FF), d ** -0.5),
        'w_ffn_down': nrm((N_EVEN, D_FF, d), D_FF ** -0.5),
        'w_in_conv': nrm((N_ODD, d, 2 * CONV_CH), d ** -0.5),
        'conv_dw': nrm((N_ODD, CONV_WIDTH, CONV_CH), CONV_WIDTH ** -0.5),
        'conv_dw_bias': nrm((N_ODD, CONV_CH), 0.01),
        'conv_ln_gain': 1.0 + nrm((N_ODD, CONV_CH), 0.1),
        'conv_ln_bias': nrm((N_ODD, CONV_CH), 0.01),
        'w_out_conv': nrm((N_ODD, CONV_CH, d), CONV_CH ** -0.5),
        'w_router': nrm((N_ODD, d, N_EXPERTS), d ** -0.5),
        'w_exp_gate': nrm((N_ODD, N_EXPERTS, d, D_FF_EXPERT), d ** -0.5),
        'w_exp_up': nrm((N_ODD, N_EXPERTS, d, D_FF_EXPERT), d ** -0.5),
        'w_exp_down': nrm((N_ODD, N_EXPERTS, D_FF_EXPERT, d), D_FF_EXPERT ** -0.5),
    }


def reference(x_prompt, x_sample, cache_k, cache_v, state_pool, state_conv, page_table,
              norm_mix, norm_ffn, norm_out, w_in_ab, sb_bias, w_pool, pool_scale, w_out_ab,
              w_ffn_gate, w_ffn_up, w_ffn_down, w_in_conv, conv_dw, conv_dw_bias,
              conv_ln_gain, conv_ln_bias, w_out_conv, w_router, w_exp_gate, w_exp_up, w_exp_down):
    p = dict(norm_mix=norm_mix, norm_ffn=norm_ffn, norm_out=norm_out, w_in_ab=w_in_ab,
             sb_bias=sb_bias, w_pool=w_pool, pool_scale=pool_scale, w_out_ab=w_out_ab,
             w_ffn_gate=w_ffn_gate, w_ffn_up=w_ffn_up, w_ffn_down=w_ffn_down, w_in_conv=w_in_conv,
             conv_dw=conv_dw, conv_dw_bias=conv_dw_bias, conv_ln_gain=conv_ln_gain,
             conv_ln_bias=conv_ln_bias, w_out_conv=w_out_conv, w_router=w_router,
             w_exp_gate=w_exp_gate, w_exp_up=w_exp_up, w_exp_down=w_exp_down)

    bp = x_prompt.shape[0]
    zero_pool = jnp.zeros((N_EVEN, bp, POOL_HIST, POOL_WIDTH), x_prompt.dtype)
    zero_conv = jnp.zeros((N_ODD, bp, CONV_HIST, CONV_CH), x_prompt.dtype)

    def attend_prompt(e, q, k, v, bias):
        return sb_attend_prompt(q, k, v, bias)

    y_prompt, prompt_k, prompt_v, prompt_pool, prompt_conv = run_trunk(
        x_prompt, 0, zero_pool, zero_conv, attend_prompt, p)

    n_seq, n_pages = page_table.shape
    past_len = n_pages * cache_k.shape[2]

    def attend_sample(e, q, k, v, bias):
        k_past = cache_k[e][page_table].reshape(n_seq, past_len, SB_HEADS, SB_HEAD_DIM)
        v_past = cache_v[e][page_table].reshape(n_seq, past_len, SB_HEADS, SB_HEAD_DIM)
        return sb_attend_cached(q, k, v, k_past, v_past, bias)

    y_sample, sample_k, sample_v, sample_pool, sample_conv = run_trunk(
        x_sample, past_len, state_pool, state_conv, attend_sample, p)

    return (y_prompt, y_sample, prompt_k, prompt_v, prompt_pool, prompt_conv,
            sample_k, sample_v, sample_pool, sample_conv)
```

```python
import functools

import jax
import jax.numpy as jnp
from jax import lax
from jax.experimental import pallas as pl
from jax.experimental.pallas import tpu as pltpu

F32 = jnp.float32
BF16 = jnp.bfloat16
U32 = jnp.uint32
I32 = jnp.int32

D = 2048
BATCH = 2
SEQ = 4096
N_SEQ = 128
DEC = 4
PAGE = 128
HEADS = 8
DH = 128
SBW = HEADS * DH
PW = D - SBW
WINDOWS = (2, 4, 8, 16)
GD = PW // len(WINDOWS)
POOL_HIST = max(WINDOWS) - 1
CONV_W = 31
CONV_HIST = CONV_W - 1
FF = 5632
N_EXP = 8
RMS_EPS = 1e-6
LN_EPS = 1e-5

MP = BATCH * SEQ
MS = N_SEQ * DEC
M = MP + MS

LANES = 128
SUBLANES = 8
VMEM_LIMIT = 56 * 1024 * 1024

TM = 512
TM_FFN = M // 8
TF = 256
TQ = 256
TP = 512
TC = 256
PAGES_PER_STEP = 4
QPAD = 16
TM_MOE = 1152
NT_MOE = (2 * M) // TM_MOE + N_EXP
CHUNK = 256


def _cparams(sem, vmem=VMEM_LIMIT):
    return pltpu.CompilerParams(dimension_semantics=sem, vmem_limit_bytes=vmem)


def _rms_bf16(x, g):
    y = x * lax.rsqrt(jnp.mean(x * x, axis=-1, keepdims=True) + RMS_EPS)
    return (y * g).astype(BF16)


def _softplus(z):
    return jnp.maximum(z, 0.0) + jnp.log1p(jnp.exp(-jnp.abs(z)))


def _split_bf16(x):
    hi = x.astype(BF16)
    lo = (x - hi.astype(F32)).astype(BF16)
    return hi, lo


def _dot(a, b):
    return jnp.dot(a, b, preferred_element_type=F32)


def _dot_nt(a, b):
    return lax.dot_general(a, b, (((1,), (1,)), ((), ())), preferred_element_type=F32)


def _pack_pair(lo, hi):
    lo_bits = pltpu.bitcast(lo.astype(BF16).astype(F32), U32)
    hi_bits = pltpu.bitcast(hi.astype(BF16).astype(F32), U32)
    return (lo_bits >> 16) | (hi_bits & jnp.uint32(0xFFFF0000))


def _unpack_pair(w):
    lo = pltpu.bitcast(w << 16, F32)
    hi = pltpu.bitcast(w & jnp.uint32(0xFFFF0000), F32)
    return lo, hi


def _inproj_kernel(x_ref, g_ref, w_ref, q_ref, k_ref, v_ref, u_ref, kb_ref, vb_ref, xn_ref):
    j = pl.program_id(1)

    @pl.when(j == 0)
    def _():
        xn_ref[...] = _rms_bf16(x_ref[...], g_ref[...])

    r = _dot(xn_ref[...], w_ref[...])

    @pl.when(j == 0)
    def _():
        q_ref[...] = r.astype(BF16)

    @pl.when(j == 1)
    def _():
        k_ref[...] = r
        kb_ref[...] = r.astype(BF16)

    @pl.when(j == 2)
    def _():
        v_ref[...] = r
        vb_ref[...] = r.astype(BF16)

    @pl.when(j == 3)
    def _():
        u_ref[...] = r


def _inproj(x, g, w):
    m = x.shape[0]
    row = pl.BlockSpec((TM, SBW), lambda i, j: (i, 0))
    return pl.pallas_call(
        _inproj_kernel,
        grid=(m // TM, 4),
        in_specs=[pl.BlockSpec((TM, D), lambda i, j: (i, 0)),
                  pl.BlockSpec((1, D), lambda i, j: (0, 0)),
                  pl.BlockSpec((D, SBW), lambda i, j: (0, j))],
        out_specs=[row] * 6,
        out_shape=[jax.ShapeDtypeStruct((m, SBW), BF16),
                   jax.ShapeDtypeStruct((m, SBW), F32),
                   jax.ShapeDtypeStruct((m, SBW), F32),
                   jax.ShapeDtypeStruct((m, PW), F32),
                   jax.ShapeDtypeStruct((m, SBW), BF16),
                   jax.ShapeDtypeStruct((m, SBW), BF16)],
        scratch_shapes=[pltpu.VMEM((TM, D), BF16)],
        compiler_params=_cparams(("parallel", "arbitrary")),
        name="inproj",
    )(x, g, w)


def _sb_block(z, valid, tri_ref, carry_ref):
    sp = _softplus(z)
    if valid is not None:
        sp = jnp.where(valid, sp, 0.0)
    hi, lo = _split_bf16(sp)
    tri = tri_ref[...]
    cs = _dot(hi, tri) + _dot(lo, tri)
    carry = carry_ref[...]
    arg = z - cs - carry
    if valid is not None:
        arg = jnp.where(valid, arg, -1e30)
    carry_ref[...] = carry + cs[:, 0:1]
    return jnp.exp(arg)


def _attn_prompt_kernel(bias_ref, q_ref, k_ref, v_ref, tri_ref, o_ref, acc_ref, carry_ref):
    h = pl.program_id(1)
    i = pl.program_id(2)
    bias = bias_ref[h]
    scale = DH ** -0.5
    q = q_ref[...]
    acc_ref[...] = jnp.zeros_like(acc_ref)
    carry_ref[...] = jnp.zeros_like(carry_ref)

    def block(kj, masked):
        start = pl.multiple_of(kj * TQ, TQ)
        ks = k_ref[pl.ds(start, TQ), :]
        vs = v_ref[pl.ds(start, TQ), :]
        z = _dot_nt(q, ks) * scale + bias
        valid = None
        if masked:
            rows = lax.broadcasted_iota(I32, (TQ, TQ), 0)
            cols = lax.broadcasted_iota(I32, (TQ, TQ), 1)
            valid = cols < rows
        a = _sb_block(z, valid, tri_ref, carry_ref)
        acc_ref[...] += _dot(a.astype(BF16), vs)

    block(i, True)

    def body(n, c):
        block(i - 1 - n, False)
        return c

    lax.fori_loop(0, i, body, 0)
    o_ref[...] = acc_ref[...].astype(BF16)


def _attn_prompt(bias, q, kb, vb, tri):
    nq = SEQ // TQ
    return pl.pallas_call(
        _attn_prompt_kernel,
        grid=(BATCH, HEADS, nq),
        in_specs=[pl.BlockSpec(memory_space=pltpu.SMEM),
                  pl.BlockSpec((TQ, DH), lambda b, h, i: (b * nq + i, h)),
                  pl.BlockSpec((SEQ, DH), lambda b, h, i: (b, h)),
                  pl.BlockSpec((SEQ, DH), lambda b, h, i: (b, h)),
                  pl.BlockSpec((TQ, TQ), lambda b, h, i: (0, 0))],
        out_specs=pl.BlockSpec((TQ, DH), lambda b, h, i: (b * nq + i, h)),
        out_shape=jax.ShapeDtypeStruct((MP, SBW), BF16),
        scratch_shapes=[pltpu.VMEM((TQ, DH), F32), pltpu.VMEM((TQ, 1), F32)],
        compiler_params=_cparams(("parallel", "parallel", "arbitrary")),
        name="attn_prompt",
    )(bias, q, kb, vb, tri)


def _attn_sample_kernel(pt_ref, q_ref, kn_ref, vn_ref, bias_ref, tri_ref, *rest):
    caches = rest[:2 * PAGES_PER_STEP]
    o_ref, acc_ref, carry_ref = rest[2 * PAGES_PER_STEP:]
    j = pl.program_id(1)
    scale = DH ** -0.5
    rows = HEADS * QPAD

    def process(k_heads, v_heads, valid):
        zs = [_dot_nt(q_ref[0, h], k_heads[h]) for h in range(HEADS)]
        z = jnp.concatenate(zs, axis=0) * scale + bias_ref[...]
        a = _sb_block(z, valid, tri_ref, carry_ref)
        outs = [_dot(a[h * QPAD:(h + 1) * QPAD].astype(BF16), v_heads[h]) for h in range(HEADS)]
        acc_ref[...] += jnp.concatenate(outs, axis=0)

    @pl.when(j == 0)
    def _():
        acc_ref[...] = jnp.zeros_like(acc_ref)
        carry_ref[...] = jnp.zeros_like(carry_ref)
        pad = jnp.zeros((PAGE - QPAD, DH), BF16)
        k_heads = [jnp.concatenate([kn_ref[0, h], pad], axis=0) for h in range(HEADS)]
        v_heads = [jnp.concatenate([vn_ref[0, h], pad], axis=0) for h in range(HEADS)]
        t = lax.broadcasted_iota(I32, (rows, PAGE), 0) % QPAD
        s = lax.broadcasted_iota(I32, (rows, PAGE), 1)
        process(k_heads, v_heads, (s < t) & (s < DEC))

    for p in range(PAGES_PER_STEP):
        ck_ref = caches[2 * p]
        cv_ref = caches[2 * p + 1]
        k_heads = [ck_ref[pl.ds(h, PAGE, stride=HEADS), :].astype(BF16) for h in range(HEADS)]
        v_heads = [cv_ref[pl.ds(h, PAGE, stride=HEADS), :].astype(BF16) for h in range(HEADS)]
        process(k_heads, v_heads, None)

    @pl.when(j == pl.num_programs(1) - 1)
    def _():
        o_ref[0] = acc_ref[...].reshape(HEADS, QPAD, DH)


def _attn_sample(page_table, qp, kn, vn, bias_rows, tri, ck2, cv2):
    n_pages = page_table.shape[1]
    steps = n_pages // PAGES_PER_STEP
    pt = page_table.reshape(-1)
    new_spec = pl.BlockSpec((1, HEADS, QPAD, DH), lambda n, j, pt: (n, 0, 0, 0))
    cache_specs = []
    for p in range(PAGES_PER_STEP):
        def idx(n, j, pt, p=p):
            return (pt[n * n_pages + n_pages - 1 - (j * PAGES_PER_STEP + p)], 0)
        cache_specs += [pl.BlockSpec((PAGE * HEADS, DH), idx)] * 2
    rows = HEADS * QPAD
    grid_spec = pltpu.PrefetchScalarGridSpec(
        num_scalar_prefetch=1,
        grid=(N_SEQ, steps),
        in_specs=[new_spec, new_spec, new_spec,
                  pl.BlockSpec((rows, PAGE), lambda n, j, pt: (0, 0)),
                  pl.BlockSpec((PAGE, PAGE), lambda n, j, pt: (0, 0))] + cache_specs,
        out_specs=pl.BlockSpec((1, HEADS, QPAD, DH), lambda n, j, pt: (n, 0, 0, 0)),
        scratch_shapes=[pltpu.VMEM((rows, DH), F32), pltpu.VMEM((rows, 1), F32)],
    )
    return pl.pallas_call(
        _attn_sample_kernel,
        grid_spec=grid_spec,
        out_shape=jax.ShapeDtypeStruct((N_SEQ, HEADS, QPAD, DH), F32),
        compiler_params=_cparams(("parallel", "arbitrary")),
        name="attn_sample",
    )(pt, qp, kn, vn, bias_rows, tri, *([ck2, cv2] * PAGES_PER_STEP))


def _pool_prompt_kernel(u_ref, w_ref, sc_ref, o_ref, full_ref):
    s = pl.program_id(1)
    halo = POOL_HIST + 1

    @pl.when(s == 0)
    def _():
        full_ref[0:halo, :] = jnp.zeros((halo, PW), F32)

    u = u_ref[...]
    full_ref[halo:halo + TP, :] = u
    pos = s * TP + lax.broadcasted_iota(I32, (TP, 1), 0)
    for g, w in enumerate(WINDOWS):
        cols = slice(g * GD, (g + 1) * GD)
        wsum = u[:, cols]
        for back in range(1, w):
            wsum = wsum + full_ref[halo - back:halo - back + TP, cols]
        count = jnp.minimum(pos + 1, w).astype(F32)
        diff = (wsum / count - u[:, cols]).astype(BF16)
        o_ref[:, cols] = (_dot(diff, w_ref[g]) * sc_ref[:, cols]).astype(BF16)
    full_ref[0:halo, :] = u[TP - halo:, :]


def _pool_prompt(u, w_pool, scale):
    ns = SEQ // TP
    return pl.pallas_call(
        _pool_prompt_kernel,
        grid=(BATCH, ns),
        in_specs=[pl.BlockSpec((TP, PW), lambda b, s: (b * ns + s, 0)),
                  pl.BlockSpec((len(WINDOWS), GD, GD), lambda b, s: (0, 0, 0)),
                  pl.BlockSpec((1, PW), lambda b, s: (0, 0))],
        out_specs=pl.BlockSpec((TP, PW), lambda b, s: (b * ns + s, 0)),
        out_shape=jax.ShapeDtypeStruct((MP, PW), BF16),
        scratch_shapes=[pltpu.VMEM((POOL_HIST + 1 + TP, PW), F32)],
        compiler_params=_cparams(("parallel", "arbitrary")),
        name="pool_prompt",
    )(u, w_pool, scale)


def _pool_sample_kernel(pos0, hist_ref, u_ref, w_ref, sc_ref, o_ref):
    def full_row(r, cols):
        if r < POOL_HIST:
            return hist_ref[:, r * PW + cols.start:r * PW + cols.stop]
        r -= POOL_HIST
        return u_ref[:, r * PW + cols.start:r * PW + cols.stop]

    for i in range(DEC):
        for g, w in enumerate(WINDOWS):
            cols = slice(g * GD, (g + 1) * GD)
            cur = full_row(POOL_HIST + i, cols)
            wsum = cur
            for back in range(1, w):
                wsum = wsum + full_row(POOL_HIST + i - back, cols)
            count = float(min(pos0 + i + 1, w))
            diff = (wsum / count - cur).astype(BF16)
            o_ref[i, :, cols] = (_dot(diff, w_ref[g]) * sc_ref[:, cols]).astype(BF16)


def _pool_sample(pos0, hist2, u2, w_pool, scale):
    return pl.pallas_call(
        functools.partial(_pool_sample_kernel, pos0),
        out_shape=jax.ShapeDtypeStruct((DEC, N_SEQ, PW), BF16),
        compiler_params=pltpu.CompilerParams(vmem_limit_bytes=VMEM_LIMIT),
        name="pool_sample",
    )(hist2, u2, w_pool, scale)


def _outproj_kernel(att_ref, pool_ref, wa_ref, wp_ref, x_ref, o_ref):
    o_ref[...] = x_ref[...] + _dot(att_ref[...], wa_ref[...]) + _dot(pool_ref[...], wp_ref[...])


def _outproj(att, pooled, w, x, buf, row0):
    m = x.shape[0]
    tn = 1024
    blk0 = row0 // TM
    in_specs = [pl.BlockSpec((TM, SBW), lambda i, j: (i, 0)),
                pl.BlockSpec((TM, PW), lambda i, j: (i, 0)),
                pl.BlockSpec((SBW, tn), lambda i, j: (0, j)),
                pl.BlockSpec((PW, tn), lambda i, j: (1, j)),
                pl.BlockSpec((TM, tn), lambda i, j: (i, j))]
    args = [att, pooled, w, w, x]
    aliases = {}
    kernel = _outproj_kernel
    if buf is not None:
        in_specs.append(pl.BlockSpec(memory_space=pl.ANY))
        args.append(buf)
        aliases = {5: 0}
        kernel = lambda a, p, wa, wp, xr, _, o: _outproj_kernel(a, p, wa, wp, xr, o)
    return pl.pallas_call(
        kernel,
        grid=(m // TM, D // tn),
        in_specs=in_specs,
        out_specs=pl.BlockSpec((TM, tn), lambda i, j: (blk0 + i, j)),
        out_shape=jax.ShapeDtypeStruct((M, D), F32),
        input_output_aliases=aliases,
        compiler_params=_cparams(("parallel", "arbitrary")),
        name="outproj",
    )(*args)


def _ffn_kernel(x_ref, g_ref, wg_ref, wu_ref, wd_ref, o_ref, xn_ref):
    f = pl.program_id(1)

    @pl.when(f == 0)
    def _():
        x = x_ref[...]
        xn_ref[...] = _rms_bf16(x, g_ref[...])
        o_ref[...] = x

    xn = xn_ref[...]
    gate = _dot(xn, wg_ref[...].astype(BF16))
    up = _dot(xn, wu_ref[...].astype(BF16))
    hid = (gate * jax.nn.sigmoid(gate) * up).astype(BF16)
    o_ref[...] += _dot(hid, wd_ref[...].astype(BF16))


def _ffn(x, g, wg, wu, wd):
    return pl.pallas_call(
        _ffn_kernel,
        grid=(M // TM_FFN, FF // TF),
        in_specs=[pl.BlockSpec((TM_FFN, D), lambda i, f: (i, 0), pipeline_mode=pl.Buffered(1)),
                  pl.BlockSpec((1, D), lambda i, f: (0, 0)),
                  pl.BlockSpec((D, TF), lambda i, f: (0, f)),
                  pl.BlockSpec((D, TF), lambda i, f: (0, f)),
                  pl.BlockSpec((TF, D), lambda i, f: (f, 0))],
        out_specs=pl.BlockSpec((TM_FFN, D), lambda i, f: (i, 0)),
        out_shape=jax.ShapeDtypeStruct((M, D), F32),
        scratch_shapes=[pltpu.VMEM((TM_FFN, D), BF16)],
        compiler_params=_cparams(("parallel", "arbitrary")),
        name="ffn",
    )(x, g, wg, wu, wd)


def _convin_kernel(x_ref, g_ref, wa_ref, wg_ref, o_ref, xn_ref):
    j = pl.program_id(1)

    @pl.when(j == 0)
    def _():
        xn_ref[...] = _rms_bf16(x_ref[...], g_ref[...])

    xn = xn_ref[...]
    a = _dot(xn, wa_ref[...])
    gate = _dot(xn, wg_ref[...])
    o_ref[...] = a * jax.nn.sigmoid(gate)


def _convin(x, g, w):
    tn = 1024
    nj = D // tn
    return pl.pallas_call(
        _convin_kernel,
        grid=(M // TM, nj),
        in_specs=[pl.BlockSpec((TM, D), lambda i, j: (i, 0)),
                  pl.BlockSpec((1, D), lambda i, j: (0, 0)),
                  pl.BlockSpec((D, tn), lambda i, j: (0, j)),
                  pl.BlockSpec((D, tn), lambda i, j: (0, nj + j))],
        out_specs=pl.BlockSpec((TM, tn), lambda i, j: (i, j)),
        out_shape=jax.ShapeDtypeStruct((M, D), F32),
        scratch_shapes=[pltpu.VMEM((TM, D), BF16)],
        compiler_params=_cparams(("parallel", "arbitrary")),
        name="convin",
    )(x, g, w, w)


def _ln_silu_bf16(h, lg, lb):
    mu = jnp.mean(h, axis=-1, keepdims=True)
    c = h - mu
    var = jnp.mean(c * c, axis=-1, keepdims=True)
    hn = c * lax.rsqrt(var + LN_EPS) * lg + lb
    return (hn * jax.nn.sigmoid(hn)).astype(BF16)


CONV_HALO = 32
CONV_RB = 64
CONV_CB = 512


def _conv_prompt_kernel(u_ref, dw_ref, db_ref, lg_ref, lb_ref, w_ref, x_ref, o_ref,
                        full_ref, hc_ref, hb_ref):
    s = pl.program_id(1)
    j = pl.program_id(2)

    @pl.when(j == 0)
    def _():
        @pl.when(s == 0)
        def _():
            full_ref[0:CONV_HALO, :] = jnp.zeros((CONV_HALO, D), F32)

        full_ref[CONV_HALO:CONV_HALO + TC, :] = u_ref[...]
        first = CONV_HALO - CONV_HIST
        for rb in range(TC // CONV_RB):
            for cb in range(D // CONV_CB):
                cols = slice(cb * CONV_CB, (cb + 1) * CONV_CB)
                acc = jnp.zeros((CONV_RB, CONV_CB), F32)
                for tap in range(CONV_W):
                    r0 = rb * CONV_RB + first + tap
                    acc = acc + full_ref[r0:r0 + CONV_RB, cols] * dw_ref[tap:tap + 1, cols]
                hc_ref[rb * CONV_RB:(rb + 1) * CONV_RB, cols] = acc + db_ref[:, cols]
        hb_ref[...] = _ln_silu_bf16(hc_ref[...], lg_ref[...], lb_ref[...])
        full_ref[0:CONV_HALO, :] = full_ref[TC:TC + CONV_HALO, :]

    o_ref[...] = x_ref[...] + _dot(hb_ref[...], w_ref[...])


def _conv_prompt(uc, dw, db, lg, lb, w, x):
    ns = SEQ // TC
    tn = 1024
    vec = pl.BlockSpec((1, D), lambda b, s, j: (0, 0))
    return pl.pallas_call(
        _conv_prompt_kernel,
        grid=(BATCH, ns, D // tn),
        in_specs=[pl.BlockSpec((TC, D), lambda b, s, j: (b * ns + s, 0)),
                  pl.BlockSpec((CONV_W, D), lambda b, s, j: (0, 0)),
                  vec, vec, vec,
                  pl.BlockSpec((D, tn), lambda b, s, j: (0, j)),
                  pl.BlockSpec((TC, tn), lambda b, s, j: (b * ns + s, j))],
        out_specs=pl.BlockSpec((TC, tn), lambda b, s, j: (b * ns + s, j)),
        out_shape=jax.ShapeDtypeStruct((M, D), F32),
        scratch_shapes=[pltpu.VMEM((CONV_HALO + TC, D), F32),
                        pltpu.VMEM((TC, D), F32),
                        pltpu.VMEM((TC, D), BF16)],
        compiler_params=_cparams(("parallel", "arbitrary", "arbitrary")),
        name="conv_prompt",
    )(uc, dw, db, lg, lb, w, x)


CONV_NB = 32


def _conv_sample_kernel(hist_ref, u_ref, dw_ref, db_ref, lg_ref, lb_ref, w_ref, x_ref, o_ref):
    def full_row(r):
        if r < CONV_HIST:
            return hist_ref[:, r * D:(r + 1) * D]
        r -= CONV_HIST
        return u_ref[:, r * D:(r + 1) * D]

    for i in range(DEC):
        acc = jnp.zeros((CONV_NB, D), F32)
        for tap in range(CONV_W):
            acc = acc + full_row(i + tap) * dw_ref[tap:tap + 1, :]
        hb = _ln_silu_bf16(acc + db_ref[...], lg_ref[...], lb_ref[...])
        o_ref[i] = x_ref[i] + _dot(hb, w_ref[...])


def _conv_sample(hist2, u2, dw, db, lg, lb, w, x3):
    vec = pl.BlockSpec((1, D), lambda n: (0, 0))
    return pl.pallas_call(
        _conv_sample_kernel,
        grid=(N_SEQ // CONV_NB,),
        in_specs=[pl.BlockSpec((CONV_NB, CONV_HIST * D), lambda n: (n, 0)),
                  pl.BlockSpec((CONV_NB, DEC * D), lambda n: (n, 0)),
                  pl.BlockSpec((CONV_W, D), lambda n: (0, 0)),
                  vec, vec, vec,
                  pl.BlockSpec((D, D), lambda n: (0, 0)),
                  pl.BlockSpec((DEC, CONV_NB, D), lambda n: (0, n, 0))],
        out_specs=pl.BlockSpec((DEC, CONV_NB, D), lambda n: (0, n, 0)),
        out_shape=jax.ShapeDtypeStruct((DEC, N_SEQ, D), F32),
        compiler_params=_cparams(("parallel",)),
        name="conv_sample",
    )(hist2, u2, dw, db, lg, lb, w, x3)


NCH = SUBLANES
HALF = D // 2


def _store_packed(ref, base_row, rows, x):
    for c in range(NCH):
        lo = x[:, c * LANES:(c + 1) * LANES]
        hi = x[:, HALF + c * LANES:HALF + (c + 1) * LANES]
        ref[pl.ds(base_row * NCH + c, rows, stride=NCH), :] = _pack_pair(lo, hi)


def _load_packed(ref, base_row, rows, dst_ref, dtype):
    for c in range(NCH):
        lo, hi = _unpack_pair(ref[pl.ds(base_row * NCH + c, rows, stride=NCH), :])
        dst_ref[:, c * LANES:(c + 1) * LANES] = lo.astype(dtype)
        dst_ref[:, HALF + c * LANES:HALF + (c + 1) * LANES] = hi.astype(dtype)


def _router_kernel(x_ref, g_ref, wr_ref, tri_ref, xp_ref, code_ref, gate_ref, cnt_ref, carry_ref):
    i = pl.program_id(0)

    @pl.when(i == 0)
    def _():
        carry_ref[...] = jnp.zeros_like(carry_ref)

    x = x_ref[...]
    xn = x * lax.rsqrt(jnp.mean(x * x, axis=-1, keepdims=True) + RMS_EPS) * g_ref[...]
    _store_packed(xp_ref, 0, TM, xn)

    logits = jnp.dot(xn, wr_ref[...], preferred_element_type=F32, precision=lax.Precision.HIGHEST)
    lane = lax.broadcasted_iota(I32, (TM, N_EXP), 1)
    m1 = jnp.max(logits, axis=-1, keepdims=True)
    i1 = jnp.min(jnp.where(logits == m1, lane, N_EXP), axis=-1, keepdims=True)
    rest = jnp.where(lane == i1, -jnp.inf, logits)
    m2 = jnp.max(rest, axis=-1, keepdims=True)
    i2 = jnp.min(jnp.where(rest == m2, lane, N_EXP), axis=-1, keepdims=True)
    e = jnp.exp(m2 - m1)
    g1 = 1.0 / (1.0 + e)
    gate_ref[:, 0:1] = g1
    gate_ref[:, 1:2] = e * g1

    hot = jnp.where((lane == i1) | (lane == i2), 1.0, 0.0)
    before = _dot(tri_ref[...], hot.astype(BF16)) + carry_ref[...]
    r1 = jnp.sum(jnp.where(lane == i1, before, 0.0), axis=-1, keepdims=True)
    r2 = jnp.sum(jnp.where(lane == i2, before, 0.0), axis=-1, keepdims=True)
    code_ref[:, 0:1] = i1
    code_ref[:, 1:2] = i2
    code_ref[:, 2:3] = r1.astype(I32)
    code_ref[:, 3:4] = r2.astype(I32)
    carry_ref[...] += jnp.sum(hot, axis=0, keepdims=True)
    cnt_ref[...] = carry_ref[...].astype(I32)


def _router(x, g, wr, tri):
    return pl.pallas_call(
        _router_kernel,
        grid=(M // TM,),
        in_specs=[pl.BlockSpec((TM, D), lambda i: (i, 0)),
                  pl.BlockSpec((1, D), lambda i: (0, 0)),
                  pl.BlockSpec((D, N_EXP), lambda i: (0, 0)),
                  pl.BlockSpec((TM, TM), lambda i: (0, 0))],
        out_specs=[pl.BlockSpec((TM * NCH, LANES), lambda i: (i, 0)),
                   pl.BlockSpec((TM, 4), lambda i: (i, 0)),
                   pl.BlockSpec((TM, 2), lambda i: (i, 0)),
                   pl.BlockSpec((1, N_EXP), lambda i: (0, 0))],
        out_shape=[jax.ShapeDtypeStruct((M * NCH, LANES), U32),
                   jax.ShapeDtypeStruct((M, 4), I32),
                   jax.ShapeDtypeStruct((M, 2), F32),
                   jax.ShapeDtypeStruct((1, N_EXP), I32)],
        scratch_shapes=[pltpu.VMEM((1, N_EXP), F32)],
        compiler_params=_cparams(("arbitrary",)),
        name="router",
    )(x, g, wr, tri)


def _dispatch_kernel(pos_ref, xp_ref, xs_ref, sem):
    def row_copy(t, k, slot):
        p = pos_ref[2 * t + k]
        return pltpu.make_async_copy(xp_ref.at[pl.ds(t * NCH, NCH), :],
                                     xs_ref.at[pl.ds(p * NCH, NCH), :], sem.at[slot])

    def wait_chunk(slot):
        rows = 2 * CHUNK * NCH
        pltpu.make_async_copy(xp_ref.at[pl.ds(0, rows), :], xs_ref.at[pl.ds(0, rows), :],
                              sem.at[slot]).wait()

    n_chunks = M // CHUNK

    def chunk(c, carry):
        slot = c % 2

        def tok(n, carry2):
            t = c * CHUNK + n
            row_copy(t, 0, slot).start()
            row_copy(t, 1, slot).start()
            return carry2

        lax.fori_loop(0, CHUNK, tok, 0)

        @pl.when(c > 0)
        def _():
            wait_chunk(1 - slot)

        return carry

    lax.fori_loop(0, n_chunks, chunk, 0)
    wait_chunk((n_chunks - 1) % 2)


def _dispatch(pos, xp):
    grid_spec = pltpu.PrefetchScalarGridSpec(
        num_scalar_prefetch=1,
        grid=(1,),
        in_specs=[pl.BlockSpec(memory_space=pl.ANY)],
        out_specs=pl.BlockSpec(memory_space=pl.ANY),
        scratch_shapes=[pltpu.SemaphoreType.DMA((2,))],
    )
    return pl.pallas_call(
        _dispatch_kernel,
        grid_spec=grid_spec,
        out_shape=jax.ShapeDtypeStruct((NT_MOE * TM_MOE * NCH, LANES), U32),
        compiler_params=pltpu.CompilerParams(dimension_semantics=("arbitrary",), has_side_effects=True),
        name="dispatch",
    )(pos, xp)


def _moe_kernel(te_ref, tn_ref, nt_ref, xs_ref, wg_ref, wu_ref, wd_ref, ys_ref, x_ref, acc_ref):
    i = pl.program_id(0)
    f = pl.program_id(1)
    active = i < nt_ref[0]

    @pl.when(active & (f == 0))
    def _():
        _load_packed(xs_ref, 0, TM_MOE, x_ref, BF16)
        rows = lax.broadcasted_iota(I32, (TM_MOE, 1), 0)
        x_ref[...] = jnp.where(rows < tn_ref[i], x_ref[...], jnp.zeros((), BF16))
        acc_ref[...] = jnp.zeros_like(acc_ref)

    @pl.when(active)
    def _():
        x = x_ref[...]
        gate = _dot(x, wg_ref[...].astype(BF16))
        up = _dot(x, wu_ref[...].astype(BF16))
        hid = (gate * jax.nn.sigmoid(gate) * up).astype(BF16)
        acc_ref[...] += _dot(hid, wd_ref[...].astype(BF16))

    @pl.when(active & (f == pl.num_programs(1) - 1))
    def _():
        _store_packed(ys_ref, 0, TM_MOE, acc_ref[...])


def _moe(tile_expert, tile_rows, n_tiles, xs, wg, wu, wd):
    nf = FF // TF

    def tile(i, nt):
        return jnp.minimum(i, nt[0] - 1)

    def fidx(i, f, nt):
        return jnp.where(i < nt[0], f, nf - 1)

    grid_spec = pltpu.PrefetchScalarGridSpec(
        num_scalar_prefetch=3,
        grid=(NT_MOE, nf),
        in_specs=[pl.BlockSpec((TM_MOE * NCH, LANES), lambda i, f, te, tn, nt: (tile(i, nt), 0)),
                  pl.BlockSpec((None, D, TF), lambda i, f, te, tn, nt: (te[tile(i, nt)], 0, fidx(i, f, nt))),
                  pl.BlockSpec((None, D, TF), lambda i, f, te, tn, nt: (te[tile(i, nt)], 0, fidx(i, f, nt))),
                  pl.BlockSpec((None, TF, D), lambda i, f, te, tn, nt: (te[tile(i, nt)], fidx(i, f, nt), 0))],
        out_specs=pl.BlockSpec((TM_MOE * NCH, LANES), lambda i, f, te, tn, nt: (tile(i, nt), 0)),
        scratch_shapes=[pltpu.VMEM((TM_MOE, D), BF16), pltpu.VMEM((TM_MOE, D), F32)],
    )
    return pl.pallas_call(
        _moe_kernel,
        grid_spec=grid_spec,
        out_shape=jax.ShapeDtypeStruct((NT_MOE * TM_MOE * NCH, LANES), U32),
        compiler_params=_cparams(("arbitrary", "arbitrary")),
        name="moe",
    )(tile_expert, tile_rows, n_tiles, xs, wg, wu, wd)


def _combine_kernel(pos_ref, ys_ref, x_ref, gate_ref, g_ref, yp_ref, ysm_ref, buf_ref, y1_ref, y2_ref, sem):
    i = pl.program_id(0)
    n = pl.num_programs(0)
    rows = 2 * TM * NCH

    def issue(tile_idx, slot):
        def tok(t, carry):
            for k in range(2):
                p = pos_ref[2 * (tile_idx * TM + t) + k]
                pltpu.make_async_copy(ys_ref.at[pl.ds(p * NCH, NCH), :],
                                      buf_ref.at[slot, pl.ds((k * TM + t) * NCH, NCH), :],
                                      sem.at[slot]).start()
            return carry

        lax.fori_loop(0, TM, tok, 0)

    @pl.when(i == 0)
    def _():
        issue(0, 0)

    @pl.when(i + 1 < n)
    def _():
        issue(i + 1, (i + 1) % 2)

    slot = i % 2
    pltpu.make_async_copy(ys_ref.at[pl.ds(0, rows), :], buf_ref.at[slot], sem.at[slot]).wait()
    _load_packed(buf_ref.at[slot], 0, TM, y1_ref, F32)
    _load_packed(buf_ref.at[slot], TM, TM, y2_ref, F32)
    moe = gate_ref[:, 0:1] * y1_ref[...] + gate_ref[:, 1:2] * y2_ref[...]
    x = x_ref[...] + moe
    y = x * lax.rsqrt(jnp.mean(x * x, axis=-1, keepdims=True) + RMS_EPS) * g_ref[...]

    @pl.when(i < MP // TM)
    def _():
        yp_ref[...] = y

    @pl.when(i >= MP // TM)
    def _():
        ysm_ref[...] = y


def _combine(pos, ys, x, gates, g):
    n_prompt = MP // TM
    grid_spec = pltpu.PrefetchScalarGridSpec(
        num_scalar_prefetch=1,
        grid=(M // TM,),
        in_specs=[pl.BlockSpec(memory_space=pl.ANY),
                  pl.BlockSpec((TM, D), lambda i, pos: (i, 0)),
                  pl.BlockSpec((TM, 2), lambda i, pos: (i, 0)),
                  pl.BlockSpec((1, D), lambda i, pos: (0, 0))],
        out_specs=[pl.BlockSpec((TM, D), lambda i, pos: (jnp.minimum(i, n_prompt - 1), 0)),
                   pl.BlockSpec((TM, D), lambda i, pos: (jnp.maximum(i - n_prompt, 0), 0))],
        scratch_shapes=[pltpu.VMEM((2, 2 * TM * NCH, LANES), U32),
                        pltpu.VMEM((TM, D), F32),
                        pltpu.VMEM((TM, D), F32),
                        pltpu.SemaphoreType.DMA((2,))],
    )
    return pl.pallas_call(
        _combine_kernel,
        grid_spec=grid_spec,
        out_shape=[jax.ShapeDtypeStruct((MP, D), F32), jax.ShapeDtypeStruct((MS, D), F32)],
        compiler_params=_cparams(("arbitrary",)),
        name="combine",
    )(pos, ys, x, gates, g)


def _moe_schedule(counts, code):
    passes = jnp.maximum((counts + TM_MOE - 1) // TM_MOE, 1)
    rows_per_pass = (counts + passes - 1) // passes
    tile_base = jnp.cumsum(passes) - passes
    n_tiles = jnp.sum(passes).astype(I32)
    tiles = jnp.arange(NT_MOE, dtype=I32)
    tile_expert = jnp.clip(jnp.searchsorted(jnp.cumsum(passes), tiles, side="right"), 0, N_EXP - 1).astype(I32)
    tile_pass = tiles - tile_base[tile_expert]
    tile_rows = jnp.clip(counts[tile_expert] - tile_pass * rows_per_pass[tile_expert], 0,
                         rows_per_pass[tile_expert]).astype(I32)
    tile_rows = jnp.where(tiles < n_tiles, tile_rows, 0)
    expert = code[:, 0:2]
    rank = code[:, 2:4]
    rpp = rows_per_pass[expert]
    pas = rank // jnp.maximum(rpp, 1)
    pos = (tile_base[expert] + pas) * TM_MOE + rank - pas * rpp
    return tile_expert, tile_rows, n_tiles.reshape(1), pos.reshape(-1).astype(I32)


def kernel(x_prompt, x_sample, cache_k, cache_v, state_pool, state_conv, page_table, norm_mix, norm_ffn, norm_out, w_in_ab, sb_bias, w_pool, pool_scale, w_out_ab, w_ffn_gate, w_ffn_up, w_ffn_down, w_in_conv, conv_dw, conv_dw_bias, conv_ln_gain, conv_ln_bias, w_out_conv, w_router, w_exp_gate, w_exp_up, w_exp_down):
    xp = x_prompt.reshape(MP, D)
    xs = x_sample.reshape(MS, D)
    past_len = page_table.shape[1] * PAGE

    w_in = w_in_ab[0].astype(BF16)
    g0 = norm_mix[0:1]
    q_p, k_p, v_p, u_p, kb_p, vb_p = _inproj(xp, g0, w_in)
    q_s, k_s, v_s, u_s, kb_s, vb_s = _inproj(xs, g0, w_in)

    tri_q = jnp.tril(jnp.ones((TQ, TQ), BF16))
    att_p = _attn_prompt(sb_bias[0], q_p, kb_p, vb_p, tri_q)

    def heads_major(t):
        t = t.reshape(N_SEQ, DEC, HEADS, DH).transpose(0, 2, 1, 3)
        return jnp.pad(t, ((0, 0), (0, 0), (0, QPAD - DEC), (0, 0)))

    tri_p = jnp.tril(jnp.ones((PAGE, PAGE), BF16))
    bias_rows = jnp.broadcast_to(jnp.repeat(sb_bias[0], QPAD)[:, None], (HEADS * QPAD, PAGE)).astype(F32)
    ck2 = cache_k[0].reshape(-1, DH)
    cv2 = cache_v[0].reshape(-1, DH)
    o_s = _attn_sample(page_table, heads_major(q_s), heads_major(kb_s), heads_major(vb_s),
                       bias_rows, tri_p, ck2, cv2)
    att_s = o_s[:, :, :DEC].transpose(0, 2, 1, 3).reshape(MS, SBW).astype(BF16)

    w_pl = w_pool[0].astype(BF16)
    psc = pool_scale[0:1]
    pooled_p = _pool_prompt(u_p, w_pl, psc)
    pooled_s = _pool_sample(past_len, state_pool[0].reshape(N_SEQ, POOL_HIST * PW),
                            u_s.reshape(N_SEQ, DEC * PW), w_pl, psc)
    pooled_s = pooled_s.transpose(1, 0, 2).reshape(MS, PW)

    w_out = w_out_ab[0].astype(BF16)
    x1 = _outproj(att_p, pooled_p, w_out, xp, None, 0)
    x1 = _outproj(att_s, pooled_s, w_out, xs, x1, MP)

    x1 = _ffn(x1, norm_ffn[0:1], w_ffn_gate[0], w_ffn_up[0], w_ffn_down[0])

    uc = _convin(x1, norm_mix[1:2], w_in_conv[0].astype(BF16))
    w_oc = w_out_conv[0].astype(BF16)
    dwb = conv_dw_bias[0:1]
    lg = conv_ln_gain[0:1]
    lb = conv_ln_bias[0:1]
    x2 = _conv_prompt(uc, conv_dw[0], dwb, lg, lb, w_oc, x1)
    uc_s = uc[MP:]
    x1_s3 = x1[MP:].reshape(N_SEQ, DEC, D).transpose(1, 0, 2)
    x2_s3 = _conv_sample(state_conv[0].reshape(N_SEQ, CONV_HIST * D), uc_s.reshape(N_SEQ, DEC * D),
                         conv_dw[0], dwb, lg, lb, w_oc, x1_s3)
    x2 = lax.dynamic_update_slice(x2, x2_s3.transpose(1, 0, 2).reshape(MS, D), (MP, 0))

    tri_r = jnp.tril(jnp.ones((TM, TM), BF16), -1)
    xpk, code, gates, counts = _router(x2, norm_ffn[1:2], w_router[0], tri_r)
    tile_expert, tile_rows, n_tiles, pos = _moe_schedule(counts[0], code)
    xsorted = _dispatch(pos, xpk)
    ysorted = _moe(tile_expert, tile_rows, n_tiles, xsorted, w_exp_gate[0], w_exp_up[0], w_exp_down[0])
    y_p, y_s = _combine(pos, ysorted, x2, gates, norm_out.reshape(1, D))

    u_p3 = u_p.reshape(BATCH, SEQ, PW)
    uc_p3 = uc[:MP].reshape(BATCH, SEQ, D)
    sample_pool = jnp.concatenate([state_pool[0][:, DEC:], u_s.reshape(N_SEQ, DEC, PW)], axis=1)
    sample_conv = jnp.concatenate([state_conv[0][:, DEC:], uc_s.reshape(N_SEQ, DEC, D)], axis=1)
    return (y_p.reshape(BATCH, SEQ, D),
            y_s.reshape(N_SEQ, DEC, D),
            k_p.reshape(1, BATCH, SEQ, HEADS, DH),
            v_p.reshape(1, BATCH, SEQ, HEADS, DH),
            u_p3[:, SEQ - POOL_HIST:][None],
            uc_p3[:, SEQ - CONV_HIST:][None],
            k_s.reshape(1, N_SEQ, DEC, HEADS, DH),
            v_s.reshape(1, N_SEQ, DEC, HEADS, DH),
            sample_pool[None],
            sample_conv[None])
```

```python
import functools

import jax
import jax.numpy as jnp
from jax import lax
from jax.experimental import pallas as pl
from jax.experimental.pallas import tpu as pltpu

F32 = jnp.float32
BF16 = jnp.bfloat16
U32 = jnp.uint32
I32 = jnp.int32

D = 2048
BATCH = 2
SEQ = 4096
N_SEQ = 128
DEC = 4
PAGE = 128
HEADS = 8
DH = 128
SBW = HEADS * DH
PW = D - SBW
WINDOWS = (2, 4, 8, 16)
GD = PW // len(WINDOWS)
POOL_HIST = max(WINDOWS) - 1
CONV_W = 31
CONV_HIST = CONV_W - 1
FF = 5632
N_EXP = 8
RMS_EPS = 1e-6
LN_EPS = 1e-5

MP = BATCH * SEQ
MS = N_SEQ * DEC
M = MP + MS

LANES = 128
SUBLANES = 8
VMEM_LIMIT = 56 * 1024 * 1024

TM = 512
TM_FFN = M // 8
TF = 256
TQ = 256
TP = 512
TC = 256
PAGES_PER_STEP = 8
QPAD = 16
TM_MOE = 1152
NT_MOE = (2 * M) // TM_MOE + N_EXP


def _cparams(sem, vmem=VMEM_LIMIT):
    return pltpu.CompilerParams(dimension_semantics=sem, vmem_limit_bytes=vmem)


def _rms_bf16(x, g):
    y = x * lax.rsqrt(jnp.mean(x * x, axis=-1, keepdims=True) + RMS_EPS)
    return (y * g).astype(BF16)


LOG2E = 1.4426950408889634
Q_SCALE = DH ** -0.5 * LOG2E


def _softplus2(z):
    neg_abs = pltpu.bitcast(pltpu.bitcast(z, U32) | jnp.uint32(0x80000000), F32)
    return jnp.maximum(z, 0.0) + jnp.log2(1.0 + jnp.exp2(neg_abs))


def _split_bf16(x):
    hi = x.astype(BF16)
    lo = (x - hi.astype(F32)).astype(BF16)
    return jnp.concatenate([hi, lo], axis=1)


def _dot(a, b):
    return jnp.dot(a, b, preferred_element_type=F32)


def _dot_nt(a, b):
    return lax.dot_general(a, b, (((1,), (1,)), ((), ())), preferred_element_type=F32)


def _pack_pair(lo, hi):
    lo_bits = pltpu.bitcast(lo.astype(BF16).astype(F32), U32)
    hi_bits = pltpu.bitcast(hi.astype(BF16).astype(F32), U32)
    return (lo_bits >> 16) | (hi_bits & jnp.uint32(0xFFFF0000))


def _unpack_pair(w):
    lo = pltpu.bitcast(w << 16, F32)
    hi = pltpu.bitcast(w & jnp.uint32(0xFFFF0000), F32)
    return lo, hi


def _inproj_kernel(x_ref, g_ref, w_ref, q_ref, k_ref, v_ref, u_ref, kb_ref, vb_ref, xn_ref):
    j = pl.program_id(1)

    @pl.when(j == 0)
    def _():
        xn_ref[...] = _rms_bf16(x_ref[...], g_ref[...])

    r = _dot(xn_ref[...], w_ref[...])

    @pl.when(j == 0)
    def _():
        q_ref[...] = (r * Q_SCALE).astype(BF16)

    @pl.when(j == 1)
    def _():
        k_ref[...] = r
        kb_ref[...] = r.astype(BF16)

    @pl.when(j == 2)
    def _():
        v_ref[...] = r
        vb_ref[...] = r.astype(BF16)

    @pl.when(j == 3)
    def _():
        u_ref[...] = r


def _inproj(x, g, w):
    m = x.shape[0]
    row = pl.BlockSpec((TM, SBW), lambda i, j: (i, 0))
    return pl.pallas_call(
        _inproj_kernel,
        grid=(m // TM, 4),
        in_specs=[pl.BlockSpec((TM, D), lambda i, j: (i, 0)),
                  pl.BlockSpec((1, D), lambda i, j: (0, 0)),
                  pl.BlockSpec((D, SBW), lambda i, j: (0, j))],
        out_specs=[row] * 6,
        out_shape=[jax.ShapeDtypeStruct((m, SBW), BF16),
                   jax.ShapeDtypeStruct((m, SBW), F32),
                   jax.ShapeDtypeStruct((m, SBW), F32),
                   jax.ShapeDtypeStruct((m, PW), F32),
                   jax.ShapeDtypeStruct((m, SBW), BF16),
                   jax.ShapeDtypeStruct((m, SBW), BF16)],
        scratch_shapes=[pltpu.VMEM((TM, D), BF16)],
        compiler_params=_cparams(("parallel", "arbitrary")),
        name="inproj",
    )(x, g, w)


def _sb_suffix_sums(z, valid, tri2_ref):
    sp = _softplus2(z)
    if valid is not None:
        sp = jnp.where(valid, sp, 0.0)
    return _dot(_split_bf16(sp), tri2_ref[...])


def _sb_weights(z, cs, carry, valid):
    arg = z - cs - carry
    if valid is not None:
        arg = jnp.where(valid, arg, -1e30)
    return jnp.exp2(arg)


def _attn_prompt_kernel(bias_ref, q_ref, k_ref, v_ref, tri_ref, o_ref, acc_ref, carry_ref):
    i = pl.program_id(1)
    acc_ref[...] = jnp.zeros_like(acc_ref)
    carry_ref[...] = jnp.zeros_like(carry_ref)

    def block(kj, masked):
        start = pl.multiple_of(kj * TQ, TQ)
        valid = None
        if masked:
            rows = lax.broadcasted_iota(I32, (TQ, TQ), 0)
            cols = lax.broadcasted_iota(I32, (TQ, TQ), 1)
            valid = cols < rows
        heads = [slice(h * DH, (h + 1) * DH) for h in range(HEADS)]
        zs = [_dot_nt(q_ref[:, c], k_ref[pl.ds(start, TQ), c]) + bias_ref[h] * LOG2E
              for h, c in enumerate(heads)]
        css = [_sb_suffix_sums(z, valid, tri_ref) for z in zs]
        ws = []
        for h in range(HEADS):
            carry = carry_ref[h]
            ws.append(_sb_weights(zs[h], css[h], carry, valid).astype(BF16))
            carry_ref[h] = carry + css[h][:, 0:1]
        for h, c in enumerate(heads):
            acc_ref[:, c] += _dot(ws[h], v_ref[pl.ds(start, TQ), c])

    block(i, True)

    def body(n, c):
        block(i - 1 - n, False)
        return c

    lax.fori_loop(0, i, body, 0)
    o_ref[...] = acc_ref[...].astype(BF16)


def _attn_prompt(bias, q, kb, vb, tri):
    nq = SEQ // TQ
    return pl.pallas_call(
        _attn_prompt_kernel,
        grid=(BATCH, nq),
        in_specs=[pl.BlockSpec(memory_space=pltpu.SMEM),
                  pl.BlockSpec((TQ, SBW), lambda b, i: (b * nq + i, 0)),
                  pl.BlockSpec((SEQ, SBW), lambda b, i: (b, 0)),
                  pl.BlockSpec((SEQ, SBW), lambda b, i: (b, 0)),
                  pl.BlockSpec((2 * TQ, TQ), lambda b, i: (0, 0))],
        out_specs=pl.BlockSpec((TQ, SBW), lambda b, i: (b * nq + i, 0)),
        out_shape=jax.ShapeDtypeStruct((MP, SBW), BF16),
        scratch_shapes=[pltpu.VMEM((TQ, SBW), F32), pltpu.VMEM((HEADS, TQ, 1), F32)],
        compiler_params=_cparams(("parallel", "arbitrary")),
        name="attn_prompt",
    )(bias, q, kb, vb, tri)


def _attn_sample_kernel(pt_ref, q_ref, kn_ref, vn_ref, bias_ref, tri_ref, *rest):
    caches = rest[:2 * PAGES_PER_STEP]
    o_ref, acc_ref, carry_ref = rest[2 * PAGES_PER_STEP:]
    j = pl.program_id(1)
    rows = HEADS * QPAD
    bias2 = bias_ref[...] * LOG2E

    def scores(k_heads):
        zs = [_dot_nt(q_ref[0, h], k_heads[h]) for h in range(HEADS)]
        return jnp.concatenate(zs, axis=0) + bias2

    def weighted(a, v_heads):
        outs = [_dot(a[h * QPAD:(h + 1) * QPAD].astype(BF16), v_heads[h]) for h in range(HEADS)]
        return jnp.concatenate(outs, axis=0)

    @pl.when(j == 0)
    def _():
        pad = jnp.zeros((PAGE - QPAD, DH), BF16)
        k_heads = [jnp.concatenate([kn_ref[0, h], pad], axis=0) for h in range(HEADS)]
        v_heads = [jnp.concatenate([vn_ref[0, h], pad], axis=0) for h in range(HEADS)]
        t = lax.broadcasted_iota(I32, (rows, PAGE), 0) % QPAD
        s = lax.broadcasted_iota(I32, (rows, PAGE), 1)
        valid = (s < t) & (s < DEC)
        z = scores(k_heads)
        cs = _sb_suffix_sums(z, valid, tri_ref)
        carry_ref[...] = cs[:, 0:1]
        acc_ref[...] = weighted(_sb_weights(z, cs, 0.0, valid), v_heads)

    def page_heads(ref):
        return [ref[pl.ds(h, PAGE, stride=HEADS), :].astype(BF16) for h in range(HEADS)]

    zs = [scores(page_heads(caches[2 * p])) for p in range(PAGES_PER_STEP)]
    css = [_sb_suffix_sums(z, None, tri_ref) for z in zs]
    carry = carry_ref[...]
    ws = []
    for p in range(PAGES_PER_STEP):
        ws.append(_sb_weights(zs[p], css[p], carry, None))
        carry = carry + css[p][:, 0:1]
    carry_ref[...] = carry
    total = weighted(ws[0], page_heads(caches[1]))
    for p in range(1, PAGES_PER_STEP):
        total = total + weighted(ws[p], page_heads(caches[2 * p + 1]))
    acc_ref[...] += total

    @pl.when(j == pl.num_programs(1) - 1)
    def _():
        o_ref[0] = acc_ref[...].reshape(HEADS, QPAD, DH)


def _attn_sample(page_table, qp, kn, vn, bias_rows, tri, ck2, cv2):
    n_pages = page_table.shape[1]
    steps = n_pages // PAGES_PER_STEP
    pt = page_table.reshape(-1)
    new_spec = pl.BlockSpec((1, HEADS, QPAD, DH), lambda n, j, pt: (n, 0, 0, 0))
    cache_specs = []
    for p in range(PAGES_PER_STEP):
        def idx(n, j, pt, p=p):
            return (pt[n * n_pages + n_pages - 1 - (j * PAGES_PER_STEP + p)], 0)
        cache_specs += [pl.BlockSpec((PAGE * HEADS, DH), idx)] * 2
    rows = HEADS * QPAD
    grid_spec = pltpu.PrefetchScalarGridSpec(
        num_scalar_prefetch=1,
        grid=(N_SEQ, steps),
        in_specs=[new_spec, new_spec, new_spec,
                  pl.BlockSpec((rows, PAGE), lambda n, j, pt: (0, 0)),
                  pl.BlockSpec((2 * PAGE, PAGE), lambda n, j, pt: (0, 0))] + cache_specs,
        out_specs=pl.BlockSpec((1, HEADS, QPAD, DH), lambda n, j, pt: (n, 0, 0, 0)),
        scratch_shapes=[pltpu.VMEM((rows, DH), F32), pltpu.VMEM((rows, 1), F32)],
    )
    return pl.pallas_call(
        _attn_sample_kernel,
        grid_spec=grid_spec,
        out_shape=jax.ShapeDtypeStruct((N_SEQ, HEADS, QPAD, DH), F32),
        compiler_params=_cparams(("parallel", "arbitrary")),
        name="attn_sample",
    )(pt, qp, kn, vn, bias_rows, tri, *([ck2, cv2] * PAGES_PER_STEP))


def _pool_prompt_kernel(u_ref, w_ref, sc_ref, o_ref, full_ref):
    s = pl.program_id(1)
    halo = POOL_HIST + 1

    @pl.when(s == 0)
    def _():
        full_ref[0:halo, :] = jnp.zeros((halo, PW), F32)

    u = u_ref[...]
    full_ref[halo:halo + TP, :] = u
    pos = s * TP + lax.broadcasted_iota(I32, (TP, 1), 0)
    for g, w in enumerate(WINDOWS):
        cols = slice(g * GD, (g + 1) * GD)
        wsum = u[:, cols]
        for back in range(1, w):
            wsum = wsum + full_ref[halo - back:halo - back + TP, cols]
        count = jnp.minimum(pos + 1, w).astype(F32)
        diff = (wsum / count - u[:, cols]).astype(BF16)
        o_ref[:, cols] = (_dot(diff, w_ref[g]) * sc_ref[:, cols]).astype(BF16)
    full_ref[0:halo, :] = u[TP - halo:, :]


def _pool_prompt(u, w_pool, scale):
    ns = SEQ // TP
    return pl.pallas_call(
        _pool_prompt_kernel,
        grid=(BATCH, ns),
        in_specs=[pl.BlockSpec((TP, PW), lambda b, s: (b * ns + s, 0)),
                  pl.BlockSpec((len(WINDOWS), GD, GD), lambda b, s: (0, 0, 0)),
                  pl.BlockSpec((1, PW), lambda b, s: (0, 0))],
        out_specs=pl.BlockSpec((TP, PW), lambda b, s: (b * ns + s, 0)),
        out_shape=jax.ShapeDtypeStruct((MP, PW), BF16),
        scratch_shapes=[pltpu.VMEM((POOL_HIST + 1 + TP, PW), F32)],
        compiler_params=_cparams(("parallel", "arbitrary")),
        name="pool_prompt",
    )(u, w_pool, scale)


def _pool_sample_kernel(pos0, hist_ref, u_ref, w_ref, sc_ref, o_ref):
    def full_row(r, cols):
        if r < POOL_HIST:
            return hist_ref[:, r * PW + cols.start:r * PW + cols.stop]
        r -= POOL_HIST
        return u_ref[:, r * PW + cols.start:r * PW + cols.stop]

    for i in range(DEC):
        for g, w in enumerate(WINDOWS):
            cols = slice(g * GD, (g + 1) * GD)
            cur = full_row(POOL_HIST + i, cols)
            wsum = cur
            for back in range(1, w):
                wsum = wsum + full_row(POOL_HIST + i - back, cols)
            count = float(min(pos0 + i + 1, w))
            diff = (wsum / count - cur).astype(BF16)
            o_ref[i, :, cols] = (_dot(diff, w_ref[g]) * sc_ref[:, cols]).astype(BF16)


def _pool_sample(pos0, hist2, u2, w_pool, scale):
    return pl.pallas_call(
        functools.partial(_pool_sample_kernel, pos0),
        out_shape=jax.ShapeDtypeStruct((DEC, N_SEQ, PW), BF16),
        compiler_params=pltpu.CompilerParams(vmem_limit_bytes=VMEM_LIMIT),
        name="pool_sample",
    )(hist2, u2, w_pool, scale)


NP_TILES = MP // TM


def _prompt_tile(i):
    return jnp.minimum(i, NP_TILES - 1)


def _sample_tile(i):
    return jnp.maximum(i - NP_TILES, 0)


def _outproj_kernel(attp_ref, atts_ref, poolp_ref, pools_ref, wa_ref, wp_ref, xp_ref, xs_ref, o_ref):
    i = pl.program_id(0)

    def project(att_ref, pool_ref, x_ref):
        o_ref[...] = x_ref[...] + _dot(att_ref[...], wa_ref[...]) + _dot(pool_ref[...], wp_ref[...])

    @pl.when(i < NP_TILES)
    def _():
        project(attp_ref, poolp_ref, xp_ref)

    @pl.when(i >= NP_TILES)
    def _():
        project(atts_ref, pools_ref, xs_ref)


def _outproj(att_p, att_s, pooled_p, pooled_s, w, x_p, x_s):
    tn = 1024
    return pl.pallas_call(
        _outproj_kernel,
        grid=(M // TM, D // tn),
        in_specs=[pl.BlockSpec((TM, SBW), lambda i, j: (_prompt_tile(i), 0)),
                  pl.BlockSpec((TM, SBW), lambda i, j: (_sample_tile(i), 0)),
                  pl.BlockSpec((TM, PW), lambda i, j: (_prompt_tile(i), 0)),
                  pl.BlockSpec((TM, PW), lambda i, j: (_sample_tile(i), 0)),
                  pl.BlockSpec((SBW, tn), lambda i, j: (0, j)),
                  pl.BlockSpec((PW, tn), lambda i, j: (1, j)),
                  pl.BlockSpec((TM, tn), lambda i, j: (_prompt_tile(i), j)),
                  pl.BlockSpec((TM, tn), lambda i, j: (_sample_tile(i), j))],
        out_specs=pl.BlockSpec((TM, tn), lambda i, j: (i, j)),
        out_shape=jax.ShapeDtypeStruct((M, D), F32),
        compiler_params=_cparams(("parallel", "arbitrary")),
        name="outproj",
    )(att_p, att_s, pooled_p, pooled_s, w, w, x_p, x_s)


def _ffn_kernel(x_ref, g_ref, wg_ref, wu_ref, wd_ref, o_ref, xn_ref):
    f = pl.program_id(1)

    @pl.when(f == 0)
    def _():
        x = x_ref[...]
        xn_ref[...] = _rms_bf16(x, g_ref[...])
        o_ref[...] = x

    xn = xn_ref[...]
    gate = _dot(xn, wg_ref[...].astype(BF16))
    up = _dot(xn, wu_ref[...].astype(BF16))
    hid = (gate * jax.nn.sigmoid(gate) * up).astype(BF16)
    o_ref[...] += _dot(hid, wd_ref[...].astype(BF16))


def _ffn(x, g, wg, wu, wd):
    return pl.pallas_call(
        _ffn_kernel,
        grid=(M // TM_FFN, FF // TF),
        in_specs=[pl.BlockSpec((TM_FFN, D), lambda i, f: (i, 0), pipeline_mode=pl.Buffered(1)),
                  pl.BlockSpec((1, D), lambda i, f: (0, 0)),
                  pl.BlockSpec((D, TF), lambda i, f: (0, f)),
                  pl.BlockSpec((D, TF), lambda i, f: (0, f)),
                  pl.BlockSpec((TF, D), lambda i, f: (f, 0))],
        out_specs=pl.BlockSpec((TM_FFN, D), lambda i, f: (i, 0)),
        out_shape=jax.ShapeDtypeStruct((M, D), F32),
        scratch_shapes=[pltpu.VMEM((TM_FFN, D), BF16)],
        compiler_params=_cparams(("parallel", "arbitrary")),
        name="ffn",
    )(x, g, wg, wu, wd)


def _convin_kernel(x_ref, g_ref, wa_ref, wg_ref, o_ref, xn_ref):
    j = pl.program_id(1)

    @pl.when(j == 0)
    def _():
        xn_ref[...] = _rms_bf16(x_ref[...], g_ref[...])

    xn = xn_ref[...]
    a = _dot(xn, wa_ref[...])
    gate = _dot(xn, wg_ref[...])
    o_ref[...] = a * jax.nn.sigmoid(gate)


def _convin(x, g, w):
    tn = 1024
    nj = D // tn
    return pl.pallas_call(
        _convin_kernel,
        grid=(M // TM, nj),
        in_specs=[pl.BlockSpec((TM, D), lambda i, j: (i, 0)),
                  pl.BlockSpec((1, D), lambda i, j: (0, 0)),
                  pl.BlockSpec((D, tn), lambda i, j: (0, j)),
                  pl.BlockSpec((D, tn), lambda i, j: (0, nj + j))],
        out_specs=pl.BlockSpec((TM, tn), lambda i, j: (i, j)),
        out_shape=jax.ShapeDtypeStruct((M, D), F32),
        scratch_shapes=[pltpu.VMEM((TM, D), BF16)],
        compiler_params=_cparams(("parallel", "arbitrary")),
        name="convin",
    )(x, g, w, w)


def _ln_silu_bf16(h, lg, lb):
    mu = jnp.mean(h, axis=-1, keepdims=True)
    c = h - mu
    var = jnp.mean(c * c, axis=-1, keepdims=True)
    hn = c * lax.rsqrt(var + LN_EPS) * lg + lb
    return (hn * jax.nn.sigmoid(hn)).astype(BF16)


CONV_HALO = 32
CONV_RB = 64
CONV_CB = 256
CONV_LEN = CONV_HALO + TC - SUBLANES


def _conv_prompt_kernel(u_ref, dw_ref, db_ref, lg_ref, lb_ref, w_ref, x_ref, o_ref,
                        full_ref, shift_ref, hc_ref, hb_ref):
    s = pl.program_id(1)
    j = pl.program_id(2)

    @pl.when(j == 0)
    def _():
        @pl.when(s == 0)
        def _():
            full_ref[0:CONV_HALO, :] = jnp.zeros((CONV_HALO, D), F32)

        full_ref[CONV_HALO:CONV_HALO + TC, :] = u_ref[...]
        for b in range(1, SUBLANES):
            shift_ref[b - 1] = full_ref[b:b + CONV_LEN, :]
        first = CONV_HALO - CONV_HIST
        for rb in range(TC // CONV_RB):
            for cb in range(D // CONV_CB):
                cols = slice(cb * CONV_CB, (cb + 1) * CONV_CB)
                acc = jnp.zeros((CONV_RB, CONV_CB), F32)
                for tap in range(CONV_W):
                    off = first + tap
                    r0 = rb * CONV_RB + (off // SUBLANES) * SUBLANES
                    if off % SUBLANES == 0:
                        rows = full_ref[r0:r0 + CONV_RB, cols]
                    else:
                        rows = shift_ref[off % SUBLANES - 1, r0:r0 + CONV_RB, cols]
                    acc = acc + rows * dw_ref[tap:tap + 1, cols]
                hc_ref[rb * CONV_RB:(rb + 1) * CONV_RB, cols] = acc + db_ref[:, cols]
        hb_ref[...] = _ln_silu_bf16(hc_ref[...], lg_ref[...], lb_ref[...])
        full_ref[0:CONV_HALO, :] = full_ref[TC:TC + CONV_HALO, :]

    o_ref[...] = x_ref[...] + _dot(hb_ref[...], w_ref[...])


def _conv_prompt(uc, dw, db, lg, lb, w, x):
    ns = SEQ // TC
    tn = 1024
    vec = pl.BlockSpec((1, D), lambda b, s, j: (0, 0))
    return pl.pallas_call(
        _conv_prompt_kernel,
        grid=(BATCH, ns, D // tn),
        in_specs=[pl.BlockSpec((TC, D), lambda b, s, j: (b * ns + s, 0)),
                  pl.BlockSpec((CONV_W, D), lambda b, s, j: (0, 0)),
                  vec, vec, vec,
                  pl.BlockSpec((D, tn), lambda b, s, j: (0, j)),
                  pl.BlockSpec((TC, tn), lambda b, s, j: (b * ns + s, j))],
        out_specs=pl.BlockSpec((TC, tn), lambda b, s, j: (b * ns + s, j)),
        out_shape=jax.ShapeDtypeStruct((MP, D), F32),
        scratch_shapes=[pltpu.VMEM((CONV_HALO + TC, D), F32),
                        pltpu.VMEM((SUBLANES - 1, CONV_LEN, D), F32),
                        pltpu.VMEM((TC, D), F32),
                        pltpu.VMEM((TC, D), BF16)],
        compiler_params=_cparams(("parallel", "arbitrary", "arbitrary")),
        name="conv_prompt",
    )(uc, dw, db, lg, lb, w, x)


CONV_NB = 32


def _conv_sample_kernel(hist_ref, u_ref, dw_ref, db_ref, lg_ref, lb_ref, w_ref, x_ref, o_ref):
    def full_row(r):
        if r < CONV_HIST:
            return hist_ref[:, r * D:(r + 1) * D]
        r -= CONV_HIST
        return u_ref[:, r * D:(r + 1) * D]

    for i in range(DEC):
        acc = jnp.zeros((CONV_NB, D), F32)
        for tap in range(CONV_W):
            acc = acc + full_row(i + tap) * dw_ref[tap:tap + 1, :]
        hb = _ln_silu_bf16(acc + db_ref[...], lg_ref[...], lb_ref[...])
        o_ref[i] = x_ref[i] + _dot(hb, w_ref[...])


def _conv_sample(hist2, u2, dw, db, lg, lb, w, x3):
    vec = pl.BlockSpec((1, D), lambda n: (0, 0))
    return pl.pallas_call(
        _conv_sample_kernel,
        grid=(N_SEQ // CONV_NB,),
        in_specs=[pl.BlockSpec((CONV_NB, CONV_HIST * D), lambda n: (n, 0)),
                  pl.BlockSpec((CONV_NB, DEC * D), lambda n: (n, 0)),
                  pl.BlockSpec((CONV_W, D), lambda n: (0, 0)),
                  vec, vec, vec,
                  pl.BlockSpec((D, D), lambda n: (0, 0)),
                  pl.BlockSpec((DEC, CONV_NB, D), lambda n: (0, n, 0))],
        out_specs=pl.BlockSpec((DEC, CONV_NB, D), lambda n: (0, n, 0)),
        out_shape=jax.ShapeDtypeStruct((DEC, N_SEQ, D), F32),
        compiler_params=_cparams(("parallel",)),
        name="conv_sample",
    )(hist2, u2, dw, db, lg, lb, w, x3)


NCH = SUBLANES
HALF = D // 2


def _store_packed(ref, base_row, rows, x):
    for c in range(NCH):
        lo = x[:, c * LANES:(c + 1) * LANES]
        hi = x[:, HALF + c * LANES:HALF + (c + 1) * LANES]
        ref[pl.ds(base_row * NCH + c, rows, stride=NCH), :] = _pack_pair(lo, hi)


def _load_packed(ref, base_row, rows, dst_ref, dtype):
    for c in range(NCH):
        lo, hi = _unpack_pair(ref[pl.ds(base_row * NCH + c, rows, stride=NCH), :])
        dst_ref[:, c * LANES:(c + 1) * LANES] = lo.astype(dtype)
        dst_ref[:, HALF + c * LANES:HALF + (c + 1) * LANES] = hi.astype(dtype)


def _router_kernel(xa_ref, xb_ref, g_ref, wr_ref, tri_ref, xp_ref, code_ref, gate_ref, cnt_ref,
                   carry_ref, x_ref):
    i = pl.program_id(0)

    @pl.when(i == 0)
    def _():
        carry_ref[...] = jnp.zeros_like(carry_ref)

    @pl.when(i < NP_TILES)
    def _():
        x_ref[...] = xa_ref[...]

    @pl.when(i >= NP_TILES)
    def _():
        x_ref[...] = xb_ref[...]

    x = x_ref[...]
    xn = x * lax.rsqrt(jnp.mean(x * x, axis=-1, keepdims=True) + RMS_EPS) * g_ref[...]
    _store_packed(xp_ref, 0, TM, xn)

    logits = jnp.dot(xn, wr_ref[...], preferred_element_type=F32, precision=lax.Precision.HIGHEST)
    lane = lax.broadcasted_iota(I32, (TM, N_EXP), 1)
    m1 = jnp.max(logits, axis=-1, keepdims=True)
    i1 = jnp.min(jnp.where(logits == m1, lane, N_EXP), axis=-1, keepdims=True)
    rest = jnp.where(lane == i1, -jnp.inf, logits)
    m2 = jnp.max(rest, axis=-1, keepdims=True)
    i2 = jnp.min(jnp.where(rest == m2, lane, N_EXP), axis=-1, keepdims=True)
    e = jnp.exp(m2 - m1)
    g1 = 1.0 / (1.0 + e)
    gate_ref[:, 0:1] = g1
    gate_ref[:, 1:2] = e * g1

    hot = jnp.where((lane == i1) | (lane == i2), 1.0, 0.0)
    before = _dot(tri_ref[...], hot.astype(BF16)) + carry_ref[...]
    r1 = jnp.sum(jnp.where(lane == i1, before, 0.0), axis=-1, keepdims=True)
    r2 = jnp.sum(jnp.where(lane == i2, before, 0.0), axis=-1, keepdims=True)
    code_ref[:, 0:1] = i1
    code_ref[:, 1:2] = i2
    code_ref[:, 2:3] = r1.astype(I32)
    code_ref[:, 3:4] = r2.astype(I32)
    carry_ref[...] += jnp.sum(hot, axis=0, keepdims=True)
    cnt_ref[...] = carry_ref[...].astype(I32)


def _router(x_p, x_s, g, wr, tri):
    return pl.pallas_call(
        _router_kernel,
        grid=(M // TM,),
        in_specs=[pl.BlockSpec((TM, D), lambda i: (_prompt_tile(i), 0)),
                  pl.BlockSpec((TM, D), lambda i: (_sample_tile(i), 0)),
                  pl.BlockSpec((1, D), lambda i: (0, 0)),
                  pl.BlockSpec((D, N_EXP), lambda i: (0, 0)),
                  pl.BlockSpec((TM, TM), lambda i: (0, 0))],
        out_specs=[pl.BlockSpec((TM * NCH, LANES), lambda i: (i, 0)),
                   pl.BlockSpec((TM, 4), lambda i: (i, 0)),
                   pl.BlockSpec((TM, 2), lambda i: (i, 0)),
                   pl.BlockSpec((1, N_EXP), lambda i: (0, 0))],
        out_shape=[jax.ShapeDtypeStruct((M * NCH, LANES), U32),
                   jax.ShapeDtypeStruct((M, 4), I32),
                   jax.ShapeDtypeStruct((M, 2), F32),
                   jax.ShapeDtypeStruct((1, N_EXP), I32)],
        scratch_shapes=[pltpu.VMEM((1, N_EXP), F32), pltpu.VMEM((TM, D), F32)],
        compiler_params=_cparams(("arbitrary",)),
        name="router",
    )(x_p, x_s, g, wr, tri)


def _dispatch_kernel(pos_ref, xp_ref, xs_ref, sem):
    i = pl.program_id(0)

    def tok(t, carry):
        for k in range(2):
            p = pos_ref[2 * (i * TM + t) + k]
            pltpu.make_async_copy(xp_ref.at[pl.ds(t * NCH, NCH), :],
                                  xs_ref.at[pl.ds(p * NCH, NCH), :], sem).start()
        return carry

    lax.fori_loop(0, TM, tok, 0)
    for _ in range(2):
        pltpu.make_async_copy(xp_ref, xs_ref.at[pl.ds(0, TM * NCH), :], sem).wait()


def _dispatch(pos, xp):
    grid_spec = pltpu.PrefetchScalarGridSpec(
        num_scalar_prefetch=1,
        grid=(M // TM,),
        in_specs=[pl.BlockSpec((TM * NCH, LANES), lambda i, pos: (i, 0))],
        out_specs=pl.BlockSpec(memory_space=pl.ANY),
        scratch_shapes=[pltpu.SemaphoreType.DMA(())],
    )
    return pl.pallas_call(
        _dispatch_kernel,
        grid_spec=grid_spec,
        out_shape=jax.ShapeDtypeStruct((NT_MOE * TM_MOE * NCH, LANES), U32),
        compiler_params=pltpu.CompilerParams(dimension_semantics=("arbitrary",), has_side_effects=True),
        name="dispatch",
    )(pos, xp)


def _moe_kernel(te_ref, tn_ref, nt_ref, xs_ref, wg_ref, wu_ref, wd_ref, ys_ref, x_ref, acc_ref):
    i = pl.program_id(0)
    f = pl.program_id(1)
    active = i < nt_ref[0]

    @pl.when(active & (f == 0))
    def _():
        _load_packed(xs_ref, 0, TM_MOE, x_ref, BF16)
        rows = lax.broadcasted_iota(I32, (TM_MOE, 1), 0)
        x_ref[...] = jnp.where(rows < tn_ref[i], x_ref[...], jnp.zeros((), BF16))
        acc_ref[...] = jnp.zeros_like(acc_ref)

    @pl.when(active)
    def _():
        x = x_ref[...]
        gate = _dot(x, wg_ref[...].astype(BF16))
        up = _dot(x, wu_ref[...].astype(BF16))
        hid = (gate * jax.nn.sigmoid(gate) * up).astype(BF16)
        acc_ref[...] += _dot(hid, wd_ref[...].astype(BF16))

    @pl.when(active & (f == pl.num_programs(1) - 1))
    def _():
        _store_packed(ys_ref, 0, TM_MOE, acc_ref[...])


def _moe(tile_expert, tile_rows, n_tiles, xs, wg, wu, wd):
    nf = FF // TF

    def tile(i, nt):
        return jnp.minimum(i, nt[0] - 1)

    def fidx(i, f, nt):
        return jnp.where(i < nt[0], f, nf - 1)

    grid_spec = pltpu.PrefetchScalarGridSpec(
        num_scalar_prefetch=3,
        grid=(NT_MOE, nf),
        in_specs=[pl.BlockSpec((TM_MOE * NCH, LANES), lambda i, f, te, tn, nt: (tile(i, nt), 0)),
                  pl.BlockSpec((None, D, TF), lambda i, f, te, tn, nt: (te[tile(i, nt)], 0, fidx(i, f, nt))),
                  pl.BlockSpec((None, D, TF), lambda i, f, te, tn, nt: (te[tile(i, nt)], 0, fidx(i, f, nt))),
                  pl.BlockSpec((None, TF, D), lambda i, f, te, tn, nt: (te[tile(i, nt)], fidx(i, f, nt), 0))],
        out_specs=pl.BlockSpec((TM_MOE * NCH, LANES), lambda i, f, te, tn, nt: (tile(i, nt), 0)),
        scratch_shapes=[pltpu.VMEM((TM_MOE, D), BF16), pltpu.VMEM((TM_MOE, D), F32)],
    )
    return pl.pallas_call(
        _moe_kernel,
        grid_spec=grid_spec,
        out_shape=jax.ShapeDtypeStruct((NT_MOE * TM_MOE * NCH, LANES), U32),
        compiler_params=_cparams(("arbitrary", "arbitrary")),
        name="moe",
    )(tile_expert, tile_rows, n_tiles, xs, wg, wu, wd)


def _combine_kernel(pos_ref, ys_ref, xa_ref, xb_ref, gate_ref, g_ref, yp_ref, ysm_ref,
                    buf_ref, y1_ref, y2_ref, sem):
    i = pl.program_id(0)
    n = pl.num_programs(0)
    rows = 2 * TM * NCH

    def issue(tile_idx, slot):
        def tok(t, carry):
            for k in range(2):
                p = pos_ref[2 * (tile_idx * TM + t) + k]
                pltpu.make_async_copy(ys_ref.at[pl.ds(p * NCH, NCH), :],
                                      buf_ref.at[slot, pl.ds((k * TM + t) * NCH, NCH), :],
                                      sem.at[slot]).start()
            return carry

        lax.fori_loop(0, TM, tok, 0)

    @pl.when(i == 0)
    def _():
        issue(0, 0)

    @pl.when(i + 1 < n)
    def _():
        issue(i + 1, (i + 1) % 2)

    slot = i % 2
    pltpu.make_async_copy(ys_ref.at[pl.ds(0, rows), :], buf_ref.at[slot], sem.at[slot]).wait()
    _load_packed(buf_ref.at[slot], 0, TM, y1_ref, F32)
    _load_packed(buf_ref.at[slot], TM, TM, y2_ref, F32)
    moe = gate_ref[:, 0:1] * y1_ref[...] + gate_ref[:, 1:2] * y2_ref[...]

    def finish(x_ref, o_ref):
        x = x_ref[...] + moe
        o_ref[...] = x * lax.rsqrt(jnp.mean(x * x, axis=-1, keepdims=True) + RMS_EPS) * g_ref[...]

    @pl.when(i < NP_TILES)
    def _():
        finish(xa_ref, yp_ref)

    @pl.when(i >= NP_TILES)
    def _():
        finish(xb_ref, ysm_ref)


def _combine(pos, ys, x_p, x_s, gates, g):
    grid_spec = pltpu.PrefetchScalarGridSpec(
        num_scalar_prefetch=1,
        grid=(M // TM,),
        in_specs=[pl.BlockSpec(memory_space=pl.ANY),
                  pl.BlockSpec((TM, D), lambda i, pos: (_prompt_tile(i), 0)),
                  pl.BlockSpec((TM, D), lambda i, pos: (_sample_tile(i), 0)),
                  pl.BlockSpec((TM, 2), lambda i, pos: (i, 0)),
                  pl.BlockSpec((1, D), lambda i, pos: (0, 0))],
        out_specs=[pl.BlockSpec((TM, D), lambda i, pos: (_prompt_tile(i), 0)),
                   pl.BlockSpec((TM, D), lambda i, pos: (_sample_tile(i), 0))],
        scratch_shapes=[pltpu.VMEM((2, 2 * TM * NCH, LANES), U32),
                        pltpu.VMEM((TM, D), F32),
                        pltpu.VMEM((TM, D), F32),
                        pltpu.SemaphoreType.DMA((2,))],
    )
    return pl.pallas_call(
        _combine_kernel,
        grid_spec=grid_spec,
        out_shape=[jax.ShapeDtypeStruct((MP, D), F32), jax.ShapeDtypeStruct((MS, D), F32)],
        compiler_params=_cparams(("arbitrary",)),
        name="combine",
    )(pos, ys, x_p, x_s, gates, g)


def _moe_schedule(counts, code):
    passes = jnp.maximum((counts + TM_MOE - 1) // TM_MOE, 1)
    rows_per_pass = (counts + passes - 1) // passes
    tile_end = jnp.cumsum(passes)
    tile_base = tile_end - passes
    n_tiles = tile_end[-1].astype(I32)
    experts = jnp.arange(N_EXP, dtype=I32)

    def per_expert(table, expert):
        return jnp.sum(jnp.where(expert[..., None] == experts, table, 0), axis=-1)

    tiles = jnp.arange(NT_MOE, dtype=I32)
    tile_expert = jnp.minimum(jnp.sum(tiles[:, None] >= tile_end[None, :], axis=1), N_EXP - 1).astype(I32)
    tile_pass = tiles - per_expert(tile_base, tile_expert)
    tile_rpp = per_expert(rows_per_pass, tile_expert)
    tile_rows = jnp.clip(per_expert(counts, tile_expert) - tile_pass * tile_rpp, 0, tile_rpp)
    tile_rows = jnp.where(tiles < n_tiles, tile_rows, 0).astype(I32)
    expert = code[:, 0:2]
    rank = code[:, 2:4]
    rpp = per_expert(rows_per_pass, expert)
    pas = rank // jnp.maximum(rpp, 1)
    pos = (per_expert(tile_base, expert) + pas) * TM_MOE + rank - pas * rpp
    return tile_expert, tile_rows, n_tiles.reshape(1), pos.reshape(-1).astype(I32)


def kernel(x_prompt, x_sample, cache_k, cache_v, state_pool, state_conv, page_table, norm_mix, norm_ffn, norm_out, w_in_ab, sb_bias, w_pool, pool_scale, w_out_ab, w_ffn_gate, w_ffn_up, w_ffn_down, w_in_conv, conv_dw, conv_dw_bias, conv_ln_gain, conv_ln_bias, w_out_conv, w_router, w_exp_gate, w_exp_up, w_exp_down):
    xp = x_prompt.reshape(MP, D)
    xs = x_sample.reshape(MS, D)
    past_len = page_table.shape[1] * PAGE

    w_in = w_in_ab[0].astype(BF16)
    g0 = norm_mix[0:1]
    q_p, k_p, v_p, u_p, kb_p, vb_p = _inproj(xp, g0, w_in)
    q_s, k_s, v_s, u_s, kb_s, vb_s = _inproj(xs, g0, w_in)

    tri_q = jnp.tile(jnp.tril(jnp.ones((TQ, TQ), BF16)), (2, 1))
    att_p = _attn_prompt(sb_bias[0], q_p, kb_p, vb_p, tri_q)

    def heads_major(t):
        t = t.reshape(N_SEQ, DEC, HEADS, DH).transpose(0, 2, 1, 3)
        return jnp.pad(t, ((0, 0), (0, 0), (0, QPAD - DEC), (0, 0)))

    tri_p = jnp.tile(jnp.tril(jnp.ones((PAGE, PAGE), BF16)), (2, 1))
    bias_rows = jnp.broadcast_to(jnp.repeat(sb_bias[0], QPAD)[:, None], (HEADS * QPAD, PAGE)).astype(F32)
    ck2 = cache_k[0].reshape(-1, DH)
    cv2 = cache_v[0].reshape(-1, DH)
    o_s = _attn_sample(page_table, heads_major(q_s), heads_major(kb_s), heads_major(vb_s),
                       bias_rows, tri_p, ck2, cv2)
    att_s = o_s[:, :, :DEC].transpose(0, 2, 1, 3).reshape(MS, SBW).astype(BF16)

    w_pl = w_pool[0].astype(BF16)
    psc = pool_scale[0:1]
    pooled_p = _pool_prompt(u_p, w_pl, psc)
    pooled_s = _pool_sample(past_len, state_pool[0].reshape(N_SEQ, POOL_HIST * PW),
                            u_s.reshape(N_SEQ, DEC * PW), w_pl, psc)
    pooled_s = pooled_s.transpose(1, 0, 2).reshape(MS, PW)

    w_out = w_out_ab[0].astype(BF16)
    x1 = _outproj(att_p, att_s, pooled_p, pooled_s, w_out, xp, xs)

    x1 = _ffn(x1, norm_ffn[0:1], w_ffn_gate[0], w_ffn_up[0], w_ffn_down[0])

    uc = _convin(x1, norm_mix[1:2], w_in_conv[0].astype(BF16))
    w_oc = w_out_conv[0].astype(BF16)
    dwb = conv_dw_bias[0:1]
    lg = conv_ln_gain[0:1]
    lb = conv_ln_bias[0:1]
    x2_p = _conv_prompt(uc, conv_dw[0], dwb, lg, lb, w_oc, x1)
    uc_s = uc[MP:]
    x1_s3 = x1[MP:].reshape(N_SEQ, DEC, D).transpose(1, 0, 2)
    x2_s3 = _conv_sample(state_conv[0].reshape(N_SEQ, CONV_HIST * D), uc_s.reshape(N_SEQ, DEC * D),
                         conv_dw[0], dwb, lg, lb, w_oc, x1_s3)
    x2_s = x2_s3.transpose(1, 0, 2).reshape(MS, D)

    tri_r = jnp.tril(jnp.ones((TM, TM), BF16), -1)
    xpk, code, gates, counts = _router(x2_p, x2_s, norm_ffn[1:2], w_router[0], tri_r)
    tile_expert, tile_rows, n_tiles, pos = _moe_schedule(counts[0], code)
    xsorted = _dispatch(pos, xpk)
    ysorted = _moe(tile_expert, tile_rows, n_tiles, xsorted, w_exp_gate[0], w_exp_up[0], w_exp_down[0])
    y_p, y_s = _combine(pos, ysorted, x2_p, x2_s, gates, norm_out.reshape(1, D))

    prompt_pool = jnp.stack([u_p[(b + 1) * SEQ - POOL_HIST:(b + 1) * SEQ] for b in range(BATCH)])
    prompt_conv = jnp.stack([uc[(b + 1) * SEQ - CONV_HIST:(b + 1) * SEQ] for b in range(BATCH)])
    sample_pool = jnp.concatenate([state_pool[0][:, DEC:], u_s.reshape(N_SEQ, DEC, PW)], axis=1)
    sample_conv = jnp.concatenate([state_conv[0][:, DEC:], uc_s.reshape(N_SEQ, DEC, D)], axis=1)
    return (y_p.reshape(BATCH, SEQ, D),
            y_s.reshape(N_SEQ, DEC, D),
            k_p.reshape(1, BATCH, SEQ, HEADS, DH),
            v_p.reshape(1, BATCH, SEQ, HEADS, DH),
            prompt_pool[None],
            prompt_conv[None],
            k_s.reshape(1, N_SEQ, DEC, HEADS, DH),
            v_s.reshape(1, N_SEQ, DEC, HEADS, DH),
            sample_pool[None],
            sample_conv[None])
```

```python
import functools

import jax
import jax.numpy as jnp
from jax import lax
from jax.experimental import pallas as pl
from jax.experimental.pallas import tpu as pltpu

F32 = jnp.float32
BF16 = jnp.bfloat16
U32 = jnp.uint32
I32 = jnp.int32

D = 2048
BATCH = 2
SEQ = 4096
N_SEQ = 128
DEC = 4
PAGE = 128
HEADS = 8
DH = 128
SBW = HEADS * DH
PW = D - SBW
WINDOWS = (2, 4, 8, 16)
GD = PW // len(WINDOWS)
POOL_HIST = max(WINDOWS) - 1
CONV_W = 31
CONV_HIST = CONV_W - 1
FF = 5632
N_EXP = 8
RMS_EPS = 1e-6
LN_EPS = 1e-5

MP = BATCH * SEQ
MS = N_SEQ * DEC
M = MP + MS

LANES = 128
SUBLANES = 8
VMEM_LIMIT = 56 * 1024 * 1024

TM = 512
TM_FFN = M // 8
TF = 256
TQ = 256
TP = 512
TC = 256
PAGES_PER_STEP = 8
QPAD = 16
TM_MOE = 1152
NT_MOE = (2 * M) // TM_MOE + N_EXP


def _cparams(sem, vmem=VMEM_LIMIT):
    return pltpu.CompilerParams(dimension_semantics=sem, vmem_limit_bytes=vmem)


def _rms_bf16(x, g):
    y = x * lax.rsqrt(jnp.mean(x * x, axis=-1, keepdims=True) + RMS_EPS)
    return (y * g).astype(BF16)


LOG2E = 1.4426950408889634
Q_SCALE = DH ** -0.5 * LOG2E


def _softplus2(z):
    neg_abs = pltpu.bitcast(pltpu.bitcast(z, U32) | jnp.uint32(0x80000000), F32)
    return jnp.maximum(z, 0.0) + jnp.log2(1.0 + jnp.exp2(neg_abs))


def _split_bf16(x):
    hi = x.astype(BF16)
    lo = (x - hi.astype(F32)).astype(BF16)
    return jnp.concatenate([hi, lo], axis=1)


def _dot(a, b):
    return jnp.dot(a, b, preferred_element_type=F32)


def _dot_nt(a, b):
    return lax.dot_general(a, b, (((1,), (1,)), ((), ())), preferred_element_type=F32)


def _pack_pair(lo, hi):
    lo_bits = pltpu.bitcast(lo.astype(BF16).astype(F32), U32)
    hi_bits = pltpu.bitcast(hi.astype(BF16).astype(F32), U32)
    return (lo_bits >> 16) | (hi_bits & jnp.uint32(0xFFFF0000))


def _unpack_pair(w):
    lo = pltpu.bitcast(w << 16, F32)
    hi = pltpu.bitcast(w & jnp.uint32(0xFFFF0000), F32)
    return lo, hi


def _inproj_kernel(x_ref, g_ref, w_ref, q_ref, k_ref, v_ref, u_ref, kb_ref, vb_ref, xn_ref):
    j = pl.program_id(1)

    @pl.when(j == 0)
    def _():
        xn_ref[...] = _rms_bf16(x_ref[...], g_ref[...])

    r = _dot(xn_ref[...], w_ref[...])

    @pl.when(j == 0)
    def _():
        q_ref[...] = (r * Q_SCALE).astype(BF16)

    @pl.when(j == 1)
    def _():
        k_ref[...] = r
        kb_ref[...] = r.astype(BF16)

    @pl.when(j == 2)
    def _():
        v_ref[...] = r
        vb_ref[...] = r.astype(BF16)

    @pl.when(j == 3)
    def _():
        u_ref[...] = r


def _inproj(x, g, w):
    m = x.shape[0]
    row = pl.BlockSpec((TM, SBW), lambda i, j: (i, 0))
    return pl.pallas_call(
        _inproj_kernel,
        grid=(m // TM, 4),
        in_specs=[pl.BlockSpec((TM, D), lambda i, j: (i, 0)),
                  pl.BlockSpec((1, D), lambda i, j: (0, 0)),
                  pl.BlockSpec((D, SBW), lambda i, j: (0, j))],
        out_specs=[row] * 6,
        out_shape=[jax.ShapeDtypeStruct((m, SBW), BF16),
                   jax.ShapeDtypeStruct((m, SBW), F32),
                   jax.ShapeDtypeStruct((m, SBW), F32),
                   jax.ShapeDtypeStruct((m, PW), F32),
                   jax.ShapeDtypeStruct((m, SBW), BF16),
                   jax.ShapeDtypeStruct((m, SBW), BF16)],
        scratch_shapes=[pltpu.VMEM((TM, D), BF16)],
        compiler_params=_cparams(("parallel", "arbitrary")),
        name="inproj",
    )(x, g, w)


def _sb_suffix_sums(z, valid, tri2_ref):
    sp = _softplus2(z)
    if valid is not None:
        sp = jnp.where(valid, sp, 0.0)
    return _dot(_split_bf16(sp), tri2_ref[...])


def _sb_weights(z, cs, carry, valid):
    arg = z - cs - carry
    if valid is not None:
        arg = jnp.where(valid, arg, -1e30)
    return jnp.exp2(arg)


def _attn_prompt_kernel(bias_ref, q_ref, k_ref, v_ref, tri_ref, o_ref, acc_ref, carry_ref):
    i = pl.program_id(1)
    acc_ref[...] = jnp.zeros_like(acc_ref)
    carry_ref[...] = jnp.zeros_like(carry_ref)

    def block(kj, masked):
        start = pl.multiple_of(kj * TQ, TQ)
        valid = None
        if masked:
            rows = lax.broadcasted_iota(I32, (TQ, TQ), 0)
            cols = lax.broadcasted_iota(I32, (TQ, TQ), 1)
            valid = cols < rows
        heads = [slice(h * DH, (h + 1) * DH) for h in range(HEADS)]
        zs = [_dot_nt(q_ref[:, c], k_ref[pl.ds(start, TQ), c]) + bias_ref[h] * LOG2E
              for h, c in enumerate(heads)]
        css = [_sb_suffix_sums(z, valid, tri_ref) for z in zs]
        ws = []
        for h in range(HEADS):
            carry = carry_ref[h]
            ws.append(_sb_weights(zs[h], css[h], carry, valid).astype(BF16))
            carry_ref[h] = carry + css[h][:, 0:1]
        for h, c in enumerate(heads):
            acc_ref[:, c] += _dot(ws[h], v_ref[pl.ds(start, TQ), c])

    block(i, True)

    def body(n, c):
        block(i - 1 - n, False)
        return c

    lax.fori_loop(0, i, body, 0)
    o_ref[...] = acc_ref[...].astype(BF16)


def _attn_prompt(bias, q, kb, vb, tri):
    nq = SEQ // TQ
    return pl.pallas_call(
        _attn_prompt_kernel,
        grid=(BATCH, nq),
        in_specs=[pl.BlockSpec(memory_space=pltpu.SMEM),
                  pl.BlockSpec((TQ, SBW), lambda b, i: (b * nq + i, 0)),
                  pl.BlockSpec((SEQ, SBW), lambda b, i: (b, 0)),
                  pl.BlockSpec((SEQ, SBW), lambda b, i: (b, 0)),
                  pl.BlockSpec((2 * TQ, TQ), lambda b, i: (0, 0))],
        out_specs=pl.BlockSpec((TQ, SBW), lambda b, i: (b * nq + i, 0)),
        out_shape=jax.ShapeDtypeStruct((MP, SBW), BF16),
        scratch_shapes=[pltpu.VMEM((TQ, SBW), F32), pltpu.VMEM((HEADS, TQ, 1), F32)],
        compiler_params=_cparams(("parallel", "arbitrary")),
        name="attn_prompt",
    )(bias, q, kb, vb, tri)


def _attn_sample_kernel(pt_ref, q_ref, kn_ref, vn_ref, bias_ref, tri_ref, *rest):
    caches = rest[:2 * PAGES_PER_STEP]
    o_ref, acc_ref, carry_ref = rest[2 * PAGES_PER_STEP:]
    j = pl.program_id(1)
    rows = HEADS * QPAD
    bias2 = bias_ref[...] * LOG2E

    def scores(k_heads):
        zs = [_dot_nt(q_ref[0, h], k_heads[h]) for h in range(HEADS)]
        return jnp.concatenate(zs, axis=0) + bias2

    def weighted(a, v_heads):
        outs = [_dot(a[h * QPAD:(h + 1) * QPAD].astype(BF16), v_heads[h]) for h in range(HEADS)]
        return jnp.concatenate(outs, axis=0)

    @pl.when(j == 0)
    def _():
        pad = jnp.zeros((PAGE - QPAD, DH), BF16)
        k_heads = [jnp.concatenate([kn_ref[0, h], pad], axis=0) for h in range(HEADS)]
        v_heads = [jnp.concatenate([vn_ref[0, h], pad], axis=0) for h in range(HEADS)]
        t = lax.broadcasted_iota(I32, (rows, PAGE), 0) % QPAD
        s = lax.broadcasted_iota(I32, (rows, PAGE), 1)
        valid = (s < t) & (s < DEC)
        z = scores(k_heads)
        cs = _sb_suffix_sums(z, valid, tri_ref)
        carry_ref[...] = cs[:, 0:1]
        acc_ref[...] = weighted(_sb_weights(z, cs, 0.0, valid), v_heads)

    def page_heads(ref):
        return [ref[pl.ds(h, PAGE, stride=HEADS), :].astype(BF16) for h in range(HEADS)]

    zs = [scores(page_heads(caches[2 * p])) for p in range(PAGES_PER_STEP)]
    css = [_sb_suffix_sums(z, None, tri_ref) for z in zs]
    carry = carry_ref[...]
    ws = []
    for p in range(PAGES_PER_STEP):
        ws.append(_sb_weights(zs[p], css[p], carry, None))
        carry = carry + css[p][:, 0:1]
    carry_ref[...] = carry
    total = weighted(ws[0], page_heads(caches[1]))
    for p in range(1, PAGES_PER_STEP):
        total = total + weighted(ws[p], page_heads(caches[2 * p + 1]))
    acc_ref[...] += total

    @pl.when(j == pl.num_programs(1) - 1)
    def _():
        o_ref[0] = acc_ref[...].reshape(HEADS, QPAD, DH)


def _attn_sample(page_table, qp, kn, vn, bias_rows, tri, ck2, cv2):
    n_pages = page_table.shape[1]
    steps = n_pages // PAGES_PER_STEP
    pt = page_table.reshape(-1)
    new_spec = pl.BlockSpec((1, HEADS, QPAD, DH), lambda n, j, pt: (n, 0, 0, 0))
    cache_specs = []
    for p in range(PAGES_PER_STEP):
        def idx(n, j, pt, p=p):
            return (pt[n * n_pages + n_pages - 1 - (j * PAGES_PER_STEP + p)], 0)
        cache_specs += [pl.BlockSpec((PAGE * HEADS, DH), idx)] * 2
    rows = HEADS * QPAD
    grid_spec = pltpu.PrefetchScalarGridSpec(
        num_scalar_prefetch=1,
        grid=(N_SEQ, steps),
        in_specs=[new_spec, new_spec, new_spec,
                  pl.BlockSpec((rows, PAGE), lambda n, j, pt: (0, 0)),
                  pl.BlockSpec((2 * PAGE, PAGE), lambda n, j, pt: (0, 0))] + cache_specs,
        out_specs=pl.BlockSpec((1, HEADS, QPAD, DH), lambda n, j, pt: (n, 0, 0, 0)),
        scratch_shapes=[pltpu.VMEM((rows, DH), F32), pltpu.VMEM((rows, 1), F32)],
    )
    return pl.pallas_call(
        _attn_sample_kernel,
        grid_spec=grid_spec,
        out_shape=jax.ShapeDtypeStruct((N_SEQ, HEADS, QPAD, DH), F32),
        compiler_params=_cparams(("parallel", "arbitrary")),
        name="attn_sample",
    )(pt, qp, kn, vn, bias_rows, tri, *([ck2, cv2] * PAGES_PER_STEP))


def _pool_prompt_kernel(u_ref, w_ref, sc_ref, o_ref, full_ref):
    s = pl.program_id(1)
    halo = POOL_HIST + 1

    @pl.when(s == 0)
    def _():
        full_ref[0:halo, :] = jnp.zeros((halo, PW), F32)

    u = u_ref[...]
    full_ref[halo:halo + TP, :] = u
    pos = s * TP + lax.broadcasted_iota(I32, (TP, 1), 0)
    for g, w in enumerate(WINDOWS):
        cols = slice(g * GD, (g + 1) * GD)
        wsum = u[:, cols]
        for back in range(1, w):
            wsum = wsum + full_ref[halo - back:halo - back + TP, cols]
        count = jnp.minimum(pos + 1, w).astype(F32)
        diff = (wsum / count - u[:, cols]).astype(BF16)
        o_ref[:, cols] = (_dot(diff, w_ref[g]) * sc_ref[:, cols]).astype(BF16)
    full_ref[0:halo, :] = u[TP - halo:, :]


def _pool_prompt(u, w_pool, scale):
    ns = SEQ // TP
    return pl.pallas_call(
        _pool_prompt_kernel,
        grid=(BATCH, ns),
        in_specs=[pl.BlockSpec((TP, PW), lambda b, s: (b * ns + s, 0)),
                  pl.BlockSpec((len(WINDOWS), GD, GD), lambda b, s: (0, 0, 0)),
                  pl.BlockSpec((1, PW), lambda b, s: (0, 0))],
        out_specs=pl.BlockSpec((TP, PW), lambda b, s: (b * ns + s, 0)),
        out_shape=jax.ShapeDtypeStruct((MP, PW), BF16),
        scratch_shapes=[pltpu.VMEM((POOL_HIST + 1 + TP, PW), F32)],
        compiler_params=_cparams(("parallel", "arbitrary")),
        name="pool_prompt",
    )(u, w_pool, scale)


def _pool_sample_kernel(pos0, hist_ref, u_ref, w_ref, sc_ref, o_ref):
    def full_row(r, cols):
        if r < POOL_HIST:
            return hist_ref[:, r * PW + cols.start:r * PW + cols.stop]
        r -= POOL_HIST
        return u_ref[:, r * PW + cols.start:r * PW + cols.stop]

    for i in range(DEC):
        for g, w in enumerate(WINDOWS):
            cols = slice(g * GD, (g + 1) * GD)
            cur = full_row(POOL_HIST + i, cols)
            wsum = cur
            for back in range(1, w):
                wsum = wsum + full_row(POOL_HIST + i - back, cols)
            count = float(min(pos0 + i + 1, w))
            diff = (wsum / count - cur).astype(BF16)
            o_ref[i, :, cols] = (_dot(diff, w_ref[g]) * sc_ref[:, cols]).astype(BF16)


def _pool_sample(pos0, hist2, u2, w_pool, scale):
    return pl.pallas_call(
        functools.partial(_pool_sample_kernel, pos0),
        out_shape=jax.ShapeDtypeStruct((DEC, N_SEQ, PW), BF16),
        compiler_params=pltpu.CompilerParams(vmem_limit_bytes=VMEM_LIMIT),
        name="pool_sample",
    )(hist2, u2, w_pool, scale)


NP_TILES = MP // TM


def _prompt_tile(i):
    return jnp.minimum(i, NP_TILES - 1)


def _sample_tile(i):
    return jnp.maximum(i - NP_TILES, 0)


def _outproj_kernel(attp_ref, atts_ref, poolp_ref, pools_ref, wa_ref, wp_ref, xp_ref, xs_ref, o_ref):
    i = pl.program_id(0)

    def project(att_ref, pool_ref, x_ref):
        o_ref[...] = x_ref[...] + _dot(att_ref[...], wa_ref[...]) + _dot(pool_ref[...], wp_ref[...])

    @pl.when(i < NP_TILES)
    def _():
        project(attp_ref, poolp_ref, xp_ref)

    @pl.when(i >= NP_TILES)
    def _():
        project(atts_ref, pools_ref, xs_ref)


def _outproj(att_p, att_s, pooled_p, pooled_s, w, x_p, x_s):
    tn = D
    return pl.pallas_call(
        _outproj_kernel,
        grid=(M // TM, D // tn),
        in_specs=[pl.BlockSpec((TM, SBW), lambda i, j: (_prompt_tile(i), 0)),
                  pl.BlockSpec((TM, SBW), lambda i, j: (_sample_tile(i), 0)),
                  pl.BlockSpec((TM, PW), lambda i, j: (_prompt_tile(i), 0)),
                  pl.BlockSpec((TM, PW), lambda i, j: (_sample_tile(i), 0)),
                  pl.BlockSpec((SBW, tn), lambda i, j: (0, j)),
                  pl.BlockSpec((PW, tn), lambda i, j: (1, j)),
                  pl.BlockSpec((TM, tn), lambda i, j: (_prompt_tile(i), j)),
                  pl.BlockSpec((TM, tn), lambda i, j: (_sample_tile(i), j))],
        out_specs=pl.BlockSpec((TM, tn), lambda i, j: (i, j)),
        out_shape=jax.ShapeDtypeStruct((M, D), F32),
        compiler_params=_cparams(("parallel", "arbitrary")),
        name="outproj",
    )(att_p, att_s, pooled_p, pooled_s, w, w, x_p, x_s)


def _ffn_kernel(x_ref, g_ref, wg_ref, wu_ref, wd_ref, o_ref, xn_ref):
    f = pl.program_id(1)

    @pl.when(f == 0)
    def _():
        x = x_ref[...]
        xn_ref[...] = _rms_bf16(x, g_ref[...])
        o_ref[...] = x

    xn = xn_ref[...]
    gate = _dot(xn, wg_ref[...].astype(BF16))
    up = _dot(xn, wu_ref[...].astype(BF16))
    hid = (gate * jax.nn.sigmoid(gate) * up).astype(BF16)
    o_ref[...] += _dot(hid, wd_ref[...].astype(BF16))


def _ffn(x, g, wg, wu, wd):
    return pl.pallas_call(
        _ffn_kernel,
        grid=(M // TM_FFN, FF // TF),
        in_specs=[pl.BlockSpec((TM_FFN, D), lambda i, f: (i, 0), pipeline_mode=pl.Buffered(1)),
                  pl.BlockSpec((1, D), lambda i, f: (0, 0)),
                  pl.BlockSpec((D, TF), lambda i, f: (0, f)),
                  pl.BlockSpec((D, TF), lambda i, f: (0, f)),
                  pl.BlockSpec((TF, D), lambda i, f: (f, 0))],
        out_specs=pl.BlockSpec((TM_FFN, D), lambda i, f: (i, 0)),
        out_shape=jax.ShapeDtypeStruct((M, D), F32),
        scratch_shapes=[pltpu.VMEM((TM_FFN, D), BF16)],
        compiler_params=_cparams(("parallel", "arbitrary")),
        name="ffn",
    )(x, g, wg, wu, wd)


def _convin_kernel(x_ref, g_ref, wa_ref, wg_ref, o_ref, xn_ref):
    j = pl.program_id(1)

    @pl.when(j == 0)
    def _():
        xn_ref[...] = _rms_bf16(x_ref[...], g_ref[...])

    xn = xn_ref[...]
    a = _dot(xn, wa_ref[...])
    gate = _dot(xn, wg_ref[...])
    o_ref[...] = a * jax.nn.sigmoid(gate)


def _convin(x, g, w):
    tn = 1024
    nj = D // tn
    return pl.pallas_call(
        _convin_kernel,
        grid=(M // TM, nj),
        in_specs=[pl.BlockSpec((TM, D), lambda i, j: (i, 0)),
                  pl.BlockSpec((1, D), lambda i, j: (0, 0)),
                  pl.BlockSpec((D, tn), lambda i, j: (0, j)),
                  pl.BlockSpec((D, tn), lambda i, j: (0, nj + j))],
        out_specs=pl.BlockSpec((TM, tn), lambda i, j: (i, j)),
        out_shape=jax.ShapeDtypeStruct((M, D), F32),
        scratch_shapes=[pltpu.VMEM((TM, D), BF16)],
        compiler_params=_cparams(("parallel", "arbitrary")),
        name="convin",
    )(x, g, w, w)


def _ln_silu_bf16(h, lg, lb):
    mu = jnp.mean(h, axis=-1, keepdims=True)
    c = h - mu
    var = jnp.mean(c * c, axis=-1, keepdims=True)
    hn = c * lax.rsqrt(var + LN_EPS) * lg + lb
    return (hn * jax.nn.sigmoid(hn)).astype(BF16)


CONV_HALO = 32
CONV_RB = 64
CONV_CB = 256
CONV_LEN = CONV_HALO + TC - SUBLANES


def _conv_prompt_kernel(u_ref, dw_ref, db_ref, lg_ref, lb_ref, w_ref, x_ref, o_ref,
                        full_ref, shift_ref, hc_ref, hb_ref):
    s = pl.program_id(1)
    j = pl.program_id(2)

    @pl.when(j == 0)
    def _():
        @pl.when(s == 0)
        def _():
            full_ref[0:CONV_HALO, :] = jnp.zeros((CONV_HALO, D), F32)

        full_ref[CONV_HALO:CONV_HALO + TC, :] = u_ref[...]
        for b in range(1, SUBLANES):
            shift_ref[b - 1] = full_ref[b:b + CONV_LEN, :]
        first = CONV_HALO - CONV_HIST
        groups = [[a for a in range((first + CONV_W - 1) // SUBLANES + 1)
                   if 0 <= a * SUBLANES + b - first < CONV_W] for b in range(SUBLANES)]
        def row_block(rb, carry):
            base = rb * CONV_RB
            for cb in range(D // CONV_CB):
                cols = slice(cb * CONV_CB, (cb + 1) * CONV_CB)
                acc = jnp.zeros((CONV_RB, CONV_CB), F32)
                for b, tiles in enumerate(groups):
                    lo = pl.multiple_of(base + tiles[0] * SUBLANES, SUBLANES)
                    n = (tiles[-1] - tiles[0]) * SUBLANES + CONV_RB
                    window = (full_ref[pl.ds(lo, n), cols] if b == 0
                              else shift_ref[b - 1, pl.ds(lo, n), cols])
                    for a in tiles:
                        tap = a * SUBLANES + b - first
                        r = (a - tiles[0]) * SUBLANES
                        taps = jnp.concatenate([dw_ref[tap, :, cols]] * (CONV_RB // SUBLANES), axis=0)
                        acc = acc + window[r:r + CONV_RB] * taps
                hc_ref[pl.ds(pl.multiple_of(base, CONV_RB), CONV_RB), cols] = acc + db_ref[:, cols]
            return carry

        lax.fori_loop(0, TC // CONV_RB, row_block, 0)
        hb_ref[...] = _ln_silu_bf16(hc_ref[...], lg_ref[...], lb_ref[...])
        full_ref[0:CONV_HALO, :] = full_ref[TC:TC + CONV_HALO, :]

    o_ref[...] = x_ref[...] + _dot(hb_ref[...], w_ref[...])


def _conv_prompt(uc, dw, db, lg, lb, w, x):
    ns = SEQ // TC
    tn = 1024
    vec = pl.BlockSpec((1, D), lambda b, s, j: (0, 0))
    return pl.pallas_call(
        _conv_prompt_kernel,
        grid=(BATCH, ns, D // tn),
        in_specs=[pl.BlockSpec((TC, D), lambda b, s, j: (b * ns + s, 0)),
                  pl.BlockSpec((CONV_W, SUBLANES, D), lambda b, s, j: (0, 0, 0)),
                  vec, vec, vec,
                  pl.BlockSpec((D, tn), lambda b, s, j: (0, j)),
                  pl.BlockSpec((TC, tn), lambda b, s, j: (b * ns + s, j))],
        out_specs=pl.BlockSpec((TC, tn), lambda b, s, j: (b * ns + s, j)),
        out_shape=jax.ShapeDtypeStruct((MP, D), F32),
        scratch_shapes=[pltpu.VMEM((CONV_HALO + TC, D), F32),
                        pltpu.VMEM((SUBLANES - 1, CONV_LEN, D), F32),
                        pltpu.VMEM((TC, D), F32),
                        pltpu.VMEM((TC, D), BF16)],
        compiler_params=_cparams(("parallel", "arbitrary", "arbitrary")),
        name="conv_prompt",
    )(uc, dw, db, lg, lb, w, x)


CONV_NB = 32


def _conv_sample_kernel(hist_ref, u_ref, dw_ref, db_ref, lg_ref, lb_ref, w_ref, x_ref, o_ref):
    def full_row(r):
        if r < CONV_HIST:
            return hist_ref[:, r * D:(r + 1) * D]
        r -= CONV_HIST
        return u_ref[:, r * D:(r + 1) * D]

    for i in range(DEC):
        acc = jnp.zeros((CONV_NB, D), F32)
        for tap in range(CONV_W):
            acc = acc + full_row(i + tap) * dw_ref[tap:tap + 1, :]
        hb = _ln_silu_bf16(acc + db_ref[...], lg_ref[...], lb_ref[...])
        o_ref[i] = x_ref[i] + _dot(hb, w_ref[...])


def _conv_sample(hist2, u2, dw, db, lg, lb, w, x3):
    vec = pl.BlockSpec((1, D), lambda n: (0, 0))
    return pl.pallas_call(
        _conv_sample_kernel,
        grid=(N_SEQ // CONV_NB,),
        in_specs=[pl.BlockSpec((CONV_NB, CONV_HIST * D), lambda n: (n, 0)),
                  pl.BlockSpec((CONV_NB, DEC * D), lambda n: (n, 0)),
                  pl.BlockSpec((CONV_W, D), lambda n: (0, 0)),
                  vec, vec, vec,
                  pl.BlockSpec((D, D), lambda n: (0, 0)),
                  pl.BlockSpec((DEC, CONV_NB, D), lambda n: (0, n, 0))],
        out_specs=pl.BlockSpec((DEC, CONV_NB, D), lambda n: (0, n, 0)),
        out_shape=jax.ShapeDtypeStruct((DEC, N_SEQ, D), F32),
        compiler_params=_cparams(("parallel",)),
        name="conv_sample",
    )(hist2, u2, dw, db, lg, lb, w, x3)


NCH = SUBLANES
HALF = D // 2


def _store_packed(ref, base_row, rows, x):
    for c in range(NCH):
        lo = x[:, c * LANES:(c + 1) * LANES]
        hi = x[:, HALF + c * LANES:HALF + (c + 1) * LANES]
        ref[pl.ds(base_row * NCH + c, rows, stride=NCH), :] = _pack_pair(lo, hi)


def _load_packed(ref, base_row, rows, dst_ref, dtype):
    for c in range(NCH):
        lo, hi = _unpack_pair(ref[pl.ds(base_row * NCH + c, rows, stride=NCH), :])
        dst_ref[:, c * LANES:(c + 1) * LANES] = lo.astype(dtype)
        dst_ref[:, HALF + c * LANES:HALF + (c + 1) * LANES] = hi.astype(dtype)


def _router_kernel(xa_ref, xb_ref, g_ref, wr_ref, tri_ref, xp_ref, code_ref, gate_ref, cnt_ref,
                   carry_ref, x_ref):
    i = pl.program_id(0)

    @pl.when(i == 0)
    def _():
        carry_ref[...] = jnp.zeros_like(carry_ref)

    @pl.when(i < NP_TILES)
    def _():
        x_ref[...] = xa_ref[...]

    @pl.when(i >= NP_TILES)
    def _():
        x_ref[...] = xb_ref[...]

    x = x_ref[...]
    xn = x * lax.rsqrt(jnp.mean(x * x, axis=-1, keepdims=True) + RMS_EPS) * g_ref[...]
    _store_packed(xp_ref, 0, TM, xn)

    xh = xn.astype(BF16)
    xl = (xn - xh.astype(F32)).astype(BF16)
    w = wr_ref[...]
    wh = w.astype(BF16)
    wl = (w - wh.astype(F32)).astype(BF16)
    logits = _dot(jnp.concatenate([xh, xl, xh], axis=1), jnp.concatenate([wh, wh, wl], axis=0))
    lane = lax.broadcasted_iota(I32, (TM, N_EXP), 1)
    m1 = jnp.max(logits, axis=-1, keepdims=True)
    i1 = jnp.min(jnp.where(logits == m1, lane, N_EXP), axis=-1, keepdims=True)
    rest = jnp.where(lane == i1, -jnp.inf, logits)
    m2 = jnp.max(rest, axis=-1, keepdims=True)
    i2 = jnp.min(jnp.where(rest == m2, lane, N_EXP), axis=-1, keepdims=True)
    e = jnp.exp(m2 - m1)
    g1 = 1.0 / (1.0 + e)
    gate_ref[:, 0:1] = g1
    gate_ref[:, 1:2] = e * g1

    hot = jnp.where((lane == i1) | (lane == i2), 1.0, 0.0)
    before = _dot(tri_ref[...], hot.astype(BF16)) + carry_ref[...]
    r1 = jnp.sum(jnp.where(lane == i1, before, 0.0), axis=-1, keepdims=True)
    r2 = jnp.sum(jnp.where(lane == i2, before, 0.0), axis=-1, keepdims=True)
    code_ref[:, 0:1] = i1
    code_ref[:, 1:2] = i2
    code_ref[:, 2:3] = r1.astype(I32)
    code_ref[:, 3:4] = r2.astype(I32)
    carry_ref[...] += jnp.sum(hot, axis=0, keepdims=True)
    cnt_ref[...] = carry_ref[...].astype(I32)


def _router(x_p, x_s, g, wr, tri):
    return pl.pallas_call(
        _router_kernel,
        grid=(M // TM,),
        in_specs=[pl.BlockSpec((TM, D), lambda i: (_prompt_tile(i), 0)),
                  pl.BlockSpec((TM, D), lambda i: (_sample_tile(i), 0)),
                  pl.BlockSpec((1, D), lambda i: (0, 0)),
                  pl.BlockSpec((D, N_EXP), lambda i: (0, 0)),
                  pl.BlockSpec((TM, TM), lambda i: (0, 0))],
        out_specs=[pl.BlockSpec((TM * NCH, LANES), lambda i: (i, 0)),
                   pl.BlockSpec((TM, 4), lambda i: (i, 0)),
                   pl.BlockSpec((TM, 2), lambda i: (i, 0)),
                   pl.BlockSpec((1, N_EXP), lambda i: (0, 0))],
        out_shape=[jax.ShapeDtypeStruct((M * NCH, LANES), U32),
                   jax.ShapeDtypeStruct((M, 4), I32),
                   jax.ShapeDtypeStruct((M, 2), F32),
                   jax.ShapeDtypeStruct((1, N_EXP), I32)],
        scratch_shapes=[pltpu.VMEM((1, N_EXP), F32), pltpu.VMEM((TM, D), F32)],
        compiler_params=_cparams(("arbitrary",)),
        name="router",
    )(x_p, x_s, g, wr, tri)


def _dispatch_kernel(pos_ref, xp_ref, xs_ref, sem):
    i = pl.program_id(0)

    def tok(t, carry):
        for k in range(2):
            p = pos_ref[2 * (i * TM + t) + k]
            pltpu.make_async_copy(xp_ref.at[pl.ds(t * NCH, NCH), :],
                                  xs_ref.at[pl.ds(p * NCH, NCH), :], sem).start()
        return carry

    lax.fori_loop(0, TM, tok, 0)
    for _ in range(2):
        pltpu.make_async_copy(xp_ref, xs_ref.at[pl.ds(0, TM * NCH), :], sem).wait()


def _dispatch(pos, xp):
    grid_spec = pltpu.PrefetchScalarGridSpec(
        num_scalar_prefetch=1,
        grid=(M // TM,),
        in_specs=[pl.BlockSpec((TM * NCH, LANES), lambda i, pos: (i, 0))],
        out_specs=pl.BlockSpec(memory_space=pl.ANY),
        scratch_shapes=[pltpu.SemaphoreType.DMA(())],
    )
    return pl.pallas_call(
        _dispatch_kernel,
        grid_spec=grid_spec,
        out_shape=jax.ShapeDtypeStruct((NT_MOE * TM_MOE * NCH, LANES), U32),
        compiler_params=pltpu.CompilerParams(dimension_semantics=("arbitrary",), has_side_effects=True),
        name="dispatch",
    )(pos, xp)


W_SPLIT = 2


def _moe_kernel(te_ref, tn_ref, nt_ref, xs_ref, *rest):
    wg_refs = rest[0:W_SPLIT]
    wu_refs = rest[W_SPLIT:2 * W_SPLIT]
    wd_refs = rest[2 * W_SPLIT:3 * W_SPLIT]
    ys_ref, x_ref, acc_ref = rest[3 * W_SPLIT:]
    i = pl.program_id(0)
    f = pl.program_id(1)
    active = i < nt_ref[0]

    @pl.when(active & (f == 0))
    def _():
        _load_packed(xs_ref, 0, TM_MOE, x_ref, BF16)
        rows = lax.broadcasted_iota(I32, (TM_MOE, 1), 0)
        x_ref[...] = jnp.where(rows < tn_ref[i], x_ref[...], jnp.zeros((), BF16))
        acc_ref[...] = jnp.zeros_like(acc_ref)

    @pl.when(active)
    def _():
        x = x_ref[...]
        wg = jnp.concatenate([r[...].astype(BF16) for r in wg_refs], axis=0)
        wu = jnp.concatenate([r[...].astype(BF16) for r in wu_refs], axis=0)
        wd = jnp.concatenate([r[...].astype(BF16) for r in wd_refs], axis=1)
        gate = _dot(x, wg)
        up = _dot(x, wu)
        hid = (gate * jax.nn.sigmoid(gate) * up).astype(BF16)
        acc_ref[...] += _dot(hid, wd)

    @pl.when(active & (f == pl.num_programs(1) - 1))
    def _():
        _store_packed(ys_ref, 0, TM_MOE, acc_ref[...])


def _moe(tile_expert, tile_rows, n_tiles, xs, wg, wu, wd):
    nf = FF // TF

    def tile(i, nt):
        return jnp.minimum(i, nt[0] - 1)

    def fidx(i, f, nt):
        return jnp.where(i < nt[0], f, nf - 1)

    def up_spec(k):
        return pl.BlockSpec((None, D // W_SPLIT, TF),
                            lambda i, f, te, tn, nt: (te[tile(i, nt)], k, fidx(i, f, nt)))

    def down_spec(k):
        return pl.BlockSpec((None, TF, D // W_SPLIT),
                            lambda i, f, te, tn, nt: (te[tile(i, nt)], fidx(i, f, nt), k))

    splits = range(W_SPLIT)
    grid_spec = pltpu.PrefetchScalarGridSpec(
        num_scalar_prefetch=3,
        grid=(NT_MOE, nf),
        in_specs=([pl.BlockSpec((TM_MOE * NCH, LANES), lambda i, f, te, tn, nt: (tile(i, nt), 0))]
                  + [up_spec(k) for k in splits] * 2 + [down_spec(k) for k in splits]),
        out_specs=pl.BlockSpec((TM_MOE * NCH, LANES), lambda i, f, te, tn, nt: (tile(i, nt), 0)),
        scratch_shapes=[pltpu.VMEM((TM_MOE, D), BF16), pltpu.VMEM((TM_MOE, D), F32)],
    )
    return pl.pallas_call(
        _moe_kernel,
        grid_spec=grid_spec,
        out_shape=jax.ShapeDtypeStruct((NT_MOE * TM_MOE * NCH, LANES), U32),
        compiler_params=_cparams(("arbitrary", "arbitrary")),
        name="moe",
    )(tile_expert, tile_rows, n_tiles, xs, *([wg] * W_SPLIT + [wu] * W_SPLIT + [wd] * W_SPLIT))


def _combine_kernel(pos_ref, ys_ref, xa_ref, xb_ref, gate_ref, g_ref, yp_ref, ysm_ref,
                    buf_ref, y1_ref, y2_ref, sem):
    i = pl.program_id(0)
    n = pl.num_programs(0)
    rows = 2 * TM * NCH

    def issue(tile_idx, slot):
        def tok(t, carry):
            for k in range(2):
                p = pos_ref[2 * (tile_idx * TM + t) + k]
                pltpu.make_async_copy(ys_ref.at[pl.ds(p * NCH, NCH), :],
                                      buf_ref.at[slot, pl.ds((k * TM + t) * NCH, NCH), :],
                                      sem.at[slot]).start()
            return carry

        lax.fori_loop(0, TM, tok, 0)

    @pl.when(i == 0)
    def _():
        issue(0, 0)

    @pl.when(i + 1 < n)
    def _():
        issue(i + 1, (i + 1) % 2)

    slot = i % 2
    pltpu.make_async_copy(ys_ref.at[pl.ds(0, rows), :], buf_ref.at[slot], sem.at[slot]).wait()
    _load_packed(buf_ref.at[slot], 0, TM, y1_ref, F32)
    _load_packed(buf_ref.at[slot], TM, TM, y2_ref, F32)
    moe = gate_ref[:, 0:1] * y1_ref[...] + gate_ref[:, 1:2] * y2_ref[...]

    def finish(x_ref, o_ref):
        x = x_ref[...] + moe
        o_ref[...] = x * lax.rsqrt(jnp.mean(x * x, axis=-1, keepdims=True) + RMS_EPS) * g_ref[...]

    @pl.when(i < NP_TILES)
    def _():
        finish(xa_ref, yp_ref)

    @pl.when(i >= NP_TILES)
    def _():
        finish(xb_ref, ysm_ref)


def _combine(pos, ys, x_p, x_s, gates, g):
    grid_spec = pltpu.PrefetchScalarGridSpec(
        num_scalar_prefetch=1,
        grid=(M // TM,),
        in_specs=[pl.BlockSpec(memory_space=pl.ANY),
                  pl.BlockSpec((TM, D), lambda i, pos: (_prompt_tile(i), 0)),
                  pl.BlockSpec((TM, D), lambda i, pos: (_sample_tile(i), 0)),
                  pl.BlockSpec((TM, 2), lambda i, pos: (i, 0)),
                  pl.BlockSpec((1, D), lambda i, pos: (0, 0))],
        out_specs=[pl.BlockSpec((TM, D), lambda i, pos: (_prompt_tile(i), 0)),
                   pl.BlockSpec((TM, D), lambda i, pos: (_sample_tile(i), 0))],
        scratch_shapes=[pltpu.VMEM((2, 2 * TM * NCH, LANES), U32),
                        pltpu.VMEM((TM, D), F32),
                        pltpu.VMEM((TM, D), F32),
                        pltpu.SemaphoreType.DMA((2,))],
    )
    return pl.pallas_call(
        _combine_kernel,
        grid_spec=grid_spec,
        out_shape=[jax.ShapeDtypeStruct((MP, D), F32), jax.ShapeDtypeStruct((MS, D), F32)],
        compiler_params=_cparams(("arbitrary",)),
        name="combine",
    )(pos, ys, x_p, x_s, gates, g)


def _moe_schedule(counts, code):
    passes = jnp.maximum((counts + TM_MOE - 1) // TM_MOE, 1)
    rows_per_pass = (counts + passes - 1) // passes
    tile_end = jnp.cumsum(passes)
    tile_base = tile_end - passes
    n_tiles = tile_end[-1].astype(I32)
    experts = jnp.arange(N_EXP, dtype=I32)

    def per_expert(table, expert):
        return jnp.sum(jnp.where(expert[..., None] == experts, table, 0), axis=-1)

    tiles = jnp.arange(NT_MOE, dtype=I32)
    tile_expert = jnp.minimum(jnp.sum(tiles[:, None] >= tile_end[None, :], axis=1), N_EXP - 1).astype(I32)
    tile_pass = tiles - per_expert(tile_base, tile_expert)
    tile_rpp = per_expert(rows_per_pass, tile_expert)
    tile_rows = jnp.clip(per_expert(counts, tile_expert) - tile_pass * tile_rpp, 0, tile_rpp)
    tile_rows = jnp.where(tiles < n_tiles, tile_rows, 0).astype(I32)
    expert = code[:, 0:2]
    rank = code[:, 2:4]
    rpp = per_expert(rows_per_pass, expert)
    pas = sum((rank >= m * rpp).astype(I32) for m in range(1, pl.cdiv(M, TM_MOE)))
    pos = (per_expert(tile_base, expert) + pas) * TM_MOE + rank - pas * rpp
    return tile_expert, tile_rows, n_tiles.reshape(1), pos.reshape(-1).astype(I32)


def kernel(x_prompt, x_sample, cache_k, cache_v, state_pool, state_conv, page_table, norm_mix, norm_ffn, norm_out, w_in_ab, sb_bias, w_pool, pool_scale, w_out_ab, w_ffn_gate, w_ffn_up, w_ffn_down, w_in_conv, conv_dw, conv_dw_bias, conv_ln_gain, conv_ln_bias, w_out_conv, w_router, w_exp_gate, w_exp_up, w_exp_down):
    xp = x_prompt.reshape(MP, D)
    xs = x_sample.reshape(MS, D)
    past_len = page_table.shape[1] * PAGE

    w_in = w_in_ab[0].astype(BF16)
    g0 = norm_mix[0:1]
    q_p, k_p, v_p, u_p, kb_p, vb_p = _inproj(xp, g0, w_in)
    q_s, k_s, v_s, u_s, kb_s, vb_s = _inproj(xs, g0, w_in)

    tri_q = jnp.tile(jnp.tril(jnp.ones((TQ, TQ), BF16)), (2, 1))
    att_p = _attn_prompt(sb_bias[0], q_p, kb_p, vb_p, tri_q)

    def heads_major(t):
        t = t.reshape(N_SEQ, DEC, HEADS, DH).transpose(0, 2, 1, 3)
        return jnp.pad(t, ((0, 0), (0, 0), (0, QPAD - DEC), (0, 0)))

    tri_p = jnp.tile(jnp.tril(jnp.ones((PAGE, PAGE), BF16)), (2, 1))
    bias_rows = jnp.broadcast_to(jnp.repeat(sb_bias[0], QPAD)[:, None], (HEADS * QPAD, PAGE)).astype(F32)
    ck2 = cache_k[0].reshape(-1, DH)
    cv2 = cache_v[0].reshape(-1, DH)
    o_s = _attn_sample(page_table, heads_major(q_s), heads_major(kb_s), heads_major(vb_s),
                       bias_rows, tri_p, ck2, cv2)
    att_s = o_s[:, :, :DEC].transpose(0, 2, 1, 3).reshape(MS, SBW).astype(BF16)

    w_pl = w_pool[0].astype(BF16)
    psc = pool_scale[0:1]
    pooled_p = _pool_prompt(u_p, w_pl, psc)
    pooled_s = _pool_sample(past_len, state_pool[0].reshape(N_SEQ, POOL_HIST * PW),
                            u_s.reshape(N_SEQ, DEC * PW), w_pl, psc)
    pooled_s = pooled_s.transpose(1, 0, 2).reshape(MS, PW)

    w_out = w_out_ab[0].astype(BF16)
    x1 = _outproj(att_p, att_s, pooled_p, pooled_s, w_out, xp, xs)

    x1 = _ffn(x1, norm_ffn[0:1], w_ffn_gate[0], w_ffn_up[0], w_ffn_down[0])

    uc = _convin(x1, norm_mix[1:2], w_in_conv[0].astype(BF16))
    w_oc = w_out_conv[0].astype(BF16)
    dwb = conv_dw_bias[0:1]
    lg = conv_ln_gain[0:1]
    lb = conv_ln_bias[0:1]
    dw_tiles = jnp.broadcast_to(conv_dw[0][:, None, :], (CONV_W, SUBLANES, D))
    x2_p = _conv_prompt(uc, dw_tiles, dwb, lg, lb, w_oc, x1)
    uc_s = uc[MP:]
    x1_s3 = x1[MP:].reshape(N_SEQ, DEC, D).transpose(1, 0, 2)
    x2_s3 = _conv_sample(state_conv[0].reshape(N_SEQ, CONV_HIST * D), uc_s.reshape(N_SEQ, DEC * D),
                         conv_dw[0], dwb, lg, lb, w_oc, x1_s3)
    x2_s = x2_s3.transpose(1, 0, 2).reshape(MS, D)

    tri_r = jnp.tril(jnp.ones((TM, TM), BF16), -1)
    xpk, code, gates, counts = _router(x2_p, x2_s, norm_ffn[1:2], w_router[0], tri_r)
    tile_expert, tile_rows, n_tiles, pos = _moe_schedule(counts[0], code)
    xsorted = _dispatch(pos, xpk)
    ysorted = _moe(tile_expert, tile_rows, n_tiles, xsorted, w_exp_gate[0], w_exp_up[0], w_exp_down[0])
    y_p, y_s = _combine(pos, ysorted, x2_p, x2_s, gates, norm_out.reshape(1, D))

    prompt_pool = jnp.stack([u_p[(b + 1) * SEQ - POOL_HIST:(b + 1) * SEQ] for b in range(BATCH)])
    prompt_conv = jnp.stack([uc[(b + 1) * SEQ - CONV_HIST:(b + 1) * SEQ] for b in range(BATCH)])
    sample_pool = jnp.concatenate([state_pool[0][:, DEC:], u_s.reshape(N_SEQ, DEC, PW)], axis=1)
    sample_conv = jnp.concatenate([state_conv[0][:, DEC:], uc_s.reshape(N_SEQ, DEC, D)], axis=1)
    return (y_p.reshape(BATCH, SEQ, D),
            y_s.reshape(N_SEQ, DEC, D),
            k_p.reshape(1, BATCH, SEQ, HEADS, DH),
            v_p.reshape(1, BATCH, SEQ, HEADS, DH),
            prompt_pool[None],
            prompt_conv[None],
            k_s.reshape(1, N_SEQ, DEC, HEADS, DH),
            v_s.reshape(1, N_SEQ, DEC, HEADS, DH),
            sample_pool[None],
            sample_conv[None])
```

```python
import functools

import jax
import jax.numpy as jnp
from jax import lax
from jax.experimental import pallas as pl
from jax.experimental.pallas import tpu as pltpu

F32 = jnp.float32
BF16 = jnp.bfloat16
U32 = jnp.uint32
I32 = jnp.int32

D = 2048
BATCH = 2
SEQ = 4096
N_SEQ = 128
DEC = 4
PAGE = 128
HEADS = 8
DH = 128
SBW = HEADS * DH
PW = D - SBW
WINDOWS = (2, 4, 8, 16)
GD = PW // len(WINDOWS)
POOL_HIST = max(WINDOWS) - 1
CONV_W = 31
CONV_HIST = CONV_W - 1
FF = 5632
N_EXP = 8
RMS_EPS = 1e-6
LN_EPS = 1e-5

MP = BATCH * SEQ
MS = N_SEQ * DEC
M = MP + MS

LANES = 128
SUBLANES = 8
VMEM_LIMIT = 56 * 1024 * 1024

TM = 512
TM_FFN = M // 8
TF = 256
TQ = 256
TP = 512
TC = 256
PAGES_PER_STEP = 8
QPAD = 16
TM_MOE = 1152
NT_MOE = (2 * M) // TM_MOE + N_EXP


def _cparams(sem, vmem=VMEM_LIMIT):
    return pltpu.CompilerParams(dimension_semantics=sem, vmem_limit_bytes=vmem)


def _rms_bf16(x, g):
    y = x * lax.rsqrt(jnp.mean(x * x, axis=-1, keepdims=True) + RMS_EPS)
    return (y * g).astype(BF16)


LOG2E = 1.4426950408889634
Q_SCALE = DH ** -0.5 * LOG2E


def _softplus2(z):
    neg_abs = pltpu.bitcast(pltpu.bitcast(z, U32) | jnp.uint32(0x80000000), F32)
    return jnp.maximum(z, 0.0) + jnp.log2(1.0 + jnp.exp2(neg_abs))


def _split_bf16(x):
    hi = x.astype(BF16)
    lo = (x - hi.astype(F32)).astype(BF16)
    return jnp.concatenate([hi, lo], axis=1)


def _dot(a, b):
    return jnp.dot(a, b, preferred_element_type=F32)


def _dot_nt(a, b):
    return lax.dot_general(a, b, (((1,), (1,)), ((), ())), preferred_element_type=F32)


def _pack_pair(lo, hi):
    lo_bits = pltpu.bitcast(lo.astype(BF16).astype(F32), U32)
    hi_bits = pltpu.bitcast(hi.astype(BF16).astype(F32), U32)
    return (lo_bits >> 16) | (hi_bits & jnp.uint32(0xFFFF0000))


def _unpack_pair(w):
    lo = pltpu.bitcast(w << 16, F32)
    hi = pltpu.bitcast(w & jnp.uint32(0xFFFF0000), F32)
    return lo, hi


def _inproj_kernel(x_ref, g_ref, w_ref, q_ref, k_ref, v_ref, u_ref, kb_ref, vb_ref, xn_ref):
    j = pl.program_id(1)

    @pl.when(j == 0)
    def _():
        xn_ref[...] = _rms_bf16(x_ref[...], g_ref[...])

    r = _dot(xn_ref[...], w_ref[...])

    @pl.when(j == 0)
    def _():
        q_ref[...] = (r * Q_SCALE).astype(BF16)

    @pl.when(j == 1)
    def _():
        k_ref[...] = r
        kb_ref[...] = r.astype(BF16)

    @pl.when(j == 2)
    def _():
        v_ref[...] = r
        vb_ref[...] = r.astype(BF16)

    @pl.when(j == 3)
    def _():
        u_ref[...] = r


def _inproj(x, g, w):
    m = x.shape[0]
    row = pl.BlockSpec((TM, SBW), lambda i, j: (i, 0))
    return pl.pallas_call(
        _inproj_kernel,
        grid=(m // TM, 4),
        in_specs=[pl.BlockSpec((TM, D), lambda i, j: (i, 0)),
                  pl.BlockSpec((1, D), lambda i, j: (0, 0)),
                  pl.BlockSpec((D, SBW), lambda i, j: (0, j))],
        out_specs=[row] * 6,
        out_shape=[jax.ShapeDtypeStruct((m, SBW), BF16),
                   jax.ShapeDtypeStruct((m, SBW), F32),
                   jax.ShapeDtypeStruct((m, SBW), F32),
                   jax.ShapeDtypeStruct((m, PW), F32),
                   jax.ShapeDtypeStruct((m, SBW), BF16),
                   jax.ShapeDtypeStruct((m, SBW), BF16)],
        scratch_shapes=[pltpu.VMEM((TM, D), BF16)],
        compiler_params=_cparams(("parallel", "arbitrary")),
        name="inproj",
    )(x, g, w)


def _sb_suffix_sums(z, valid, tri2_ref):
    sp = _softplus2(z)
    if valid is not None:
        sp = jnp.where(valid, sp, 0.0)
    return _dot(_split_bf16(sp), tri2_ref[...])


def _sb_weights(z, cs, carry, valid):
    arg = z - cs - carry
    if valid is not None:
        arg = jnp.where(valid, arg, -1e30)
    return jnp.exp2(arg)


def _attn_prompt_kernel(bias_ref, q_ref, k_ref, v_ref, tri_ref, o_ref, acc_ref, carry_ref):
    i = pl.program_id(1)
    acc_ref[...] = jnp.zeros_like(acc_ref)
    carry_ref[...] = jnp.zeros_like(carry_ref)

    def block(kj, masked):
        start = pl.multiple_of(kj * TQ, TQ)
        valid = None
        if masked:
            rows = lax.broadcasted_iota(I32, (TQ, TQ), 0)
            cols = lax.broadcasted_iota(I32, (TQ, TQ), 1)
            valid = cols < rows
        heads = [slice(h * DH, (h + 1) * DH) for h in range(HEADS)]
        zs = [_dot_nt(q_ref[:, c], k_ref[pl.ds(start, TQ), c]) + bias_ref[h] * LOG2E
              for h, c in enumerate(heads)]
        css = [_sb_suffix_sums(z, valid, tri_ref) for z in zs]
        ws = []
        for h in range(HEADS):
            carry = carry_ref[h]
            ws.append(_sb_weights(zs[h], css[h], carry, valid).astype(BF16))
            carry_ref[h] = carry + css[h][:, 0:1]
        for h, c in enumerate(heads):
            acc_ref[:, c] += _dot(ws[h], v_ref[pl.ds(start, TQ), c])

    block(i, True)

    def body(n, c):
        block(i - 1 - n, False)
        return c

    lax.fori_loop(0, i, body, 0)
    o_ref[...] = acc_ref[...].astype(BF16)


def _attn_prompt(bias, q, kb, vb, tri):
    nq = SEQ // TQ
    return pl.pallas_call(
        _attn_prompt_kernel,
        grid=(BATCH, nq),
        in_specs=[pl.BlockSpec(memory_space=pltpu.SMEM),
                  pl.BlockSpec((TQ, SBW), lambda b, i: (b * nq + i, 0)),
                  pl.BlockSpec((SEQ, SBW), lambda b, i: (b, 0)),
                  pl.BlockSpec((SEQ, SBW), lambda b, i: (b, 0)),
                  pl.BlockSpec((2 * TQ, TQ), lambda b, i: (0, 0))],
        out_specs=pl.BlockSpec((TQ, SBW), lambda b, i: (b * nq + i, 0)),
        out_shape=jax.ShapeDtypeStruct((MP, SBW), BF16),
        scratch_shapes=[pltpu.VMEM((TQ, SBW), F32), pltpu.VMEM((HEADS, TQ, 1), F32)],
        compiler_params=_cparams(("parallel", "arbitrary")),
        name="attn_prompt",
    )(bias, q, kb, vb, tri)


PAGE_ROWS = PAGE * HEADS


def _attn_sample_kernel(n_pages, pt_ref, q_ref, kn_ref, vn_ref, bias_ref, tri_ref, ck_ref, cv_ref,
                        o_ref, acc_ref, carry_ref, kbuf_ref, vbuf_ref, sem):
    n = pl.program_id(0)
    j = pl.program_id(1)
    steps = n_pages // PAGES_PER_STEP
    rows = HEADS * QPAD
    bias2 = bias_ref[...] * LOG2E
    step = n * steps + j
    slot = step % 2

    def fetch(step_idx, dst_slot):
        seq = step_idx // steps
        first = (step_idx % steps) * PAGES_PER_STEP
        for p in range(PAGES_PER_STEP):
            page = pt_ref[seq * n_pages + n_pages - 1 - (first + p)]
            src = pl.ds(page * PAGE_ROWS, PAGE_ROWS)
            dst = pl.ds(p * PAGE_ROWS, PAGE_ROWS)
            pltpu.make_async_copy(ck_ref.at[src, :], kbuf_ref.at[dst_slot, dst, :], sem.at[dst_slot]).start()
            pltpu.make_async_copy(cv_ref.at[src, :], vbuf_ref.at[dst_slot, dst, :], sem.at[dst_slot]).start()

    @pl.when(step == 0)
    def _():
        fetch(0, 0)

    @pl.when(step + 1 < N_SEQ * steps)
    def _():
        fetch(step + 1, 1 - slot)

    whole = pl.ds(0, PAGES_PER_STEP * PAGE_ROWS)
    pltpu.make_async_copy(ck_ref.at[whole, :], kbuf_ref.at[slot], sem.at[slot]).wait()
    pltpu.make_async_copy(cv_ref.at[whole, :], vbuf_ref.at[slot], sem.at[slot]).wait()

    def scores(k_heads):
        zs = [_dot_nt(q_ref[0, h], k_heads[h]) for h in range(HEADS)]
        return jnp.concatenate(zs, axis=0) + bias2

    def weighted(a, v_heads):
        outs = [_dot(a[h * QPAD:(h + 1) * QPAD].astype(BF16), v_heads[h]) for h in range(HEADS)]
        return jnp.concatenate(outs, axis=0)

    @pl.when(j == 0)
    def _():
        pad = jnp.zeros((PAGE - QPAD, DH), BF16)
        k_heads = [jnp.concatenate([kn_ref[0, h], pad], axis=0) for h in range(HEADS)]
        v_heads = [jnp.concatenate([vn_ref[0, h], pad], axis=0) for h in range(HEADS)]
        t = lax.broadcasted_iota(I32, (rows, PAGE), 0) % QPAD
        s = lax.broadcasted_iota(I32, (rows, PAGE), 1)
        valid = (s < t) & (s < DEC)
        z = scores(k_heads)
        cs = _sb_suffix_sums(z, valid, tri_ref)
        carry_ref[...] = cs[:, 0:1]
        acc_ref[...] = weighted(_sb_weights(z, cs, 0.0, valid), v_heads)

    def page_heads(buf_ref, p):
        return [buf_ref[slot, pl.ds(p * PAGE_ROWS + h, PAGE, stride=HEADS), :].astype(BF16)
                for h in range(HEADS)]

    zs = [scores(page_heads(kbuf_ref, p)) for p in range(PAGES_PER_STEP)]
    css = [_sb_suffix_sums(z, None, tri_ref) for z in zs]
    carry = carry_ref[...]
    ws = []
    for p in range(PAGES_PER_STEP):
        ws.append(_sb_weights(zs[p], css[p], carry, None))
        carry = carry + css[p][:, 0:1]
    carry_ref[...] = carry
    total = weighted(ws[0], page_heads(vbuf_ref, 0))
    for p in range(1, PAGES_PER_STEP):
        total = total + weighted(ws[p], page_heads(vbuf_ref, p))
    acc_ref[...] += total

    @pl.when(j == steps - 1)
    def _():
        o_ref[0] = acc_ref[...].reshape(HEADS, QPAD, DH)


def _attn_sample(page_table, qp, kn, vn, bias_rows, tri, ck2, cv2):
    n_pages = page_table.shape[1]
    steps = n_pages // PAGES_PER_STEP
    pt = page_table.reshape(-1)
    new_spec = pl.BlockSpec((1, HEADS, QPAD, DH), lambda n, j, pt: (n, 0, 0, 0))
    rows = HEADS * QPAD
    buf = pltpu.VMEM((2, PAGES_PER_STEP * PAGE_ROWS, DH), F32)
    grid_spec = pltpu.PrefetchScalarGridSpec(
        num_scalar_prefetch=1,
        grid=(N_SEQ, steps),
        in_specs=[new_spec, new_spec, new_spec,
                  pl.BlockSpec((rows, PAGE), lambda n, j, pt: (0, 0)),
                  pl.BlockSpec((2 * PAGE, PAGE), lambda n, j, pt: (0, 0)),
                  pl.BlockSpec(memory_space=pl.ANY),
                  pl.BlockSpec(memory_space=pl.ANY)],
        out_specs=pl.BlockSpec((1, HEADS, QPAD, DH), lambda n, j, pt: (n, 0, 0, 0)),
        scratch_shapes=[pltpu.VMEM((rows, DH), F32), pltpu.VMEM((rows, 1), F32),
                        buf, buf, pltpu.SemaphoreType.DMA((2,))],
    )
    return pl.pallas_call(
        functools.partial(_attn_sample_kernel, n_pages),
        grid_spec=grid_spec,
        out_shape=jax.ShapeDtypeStruct((N_SEQ, HEADS, QPAD, DH), F32),
        compiler_params=_cparams(("arbitrary", "arbitrary")),
        name="attn_sample",
    )(pt, qp, kn, vn, bias_rows, tri, ck2, cv2)


def _pool_prompt_kernel(u_ref, w_ref, sc_ref, o_ref, full_ref):
    s = pl.program_id(1)
    halo = POOL_HIST + 1

    @pl.when(s == 0)
    def _():
        full_ref[0:halo, :] = jnp.zeros((halo, PW), F32)

    u = u_ref[...]
    full_ref[halo:halo + TP, :] = u
    pos = s * TP + lax.broadcasted_iota(I32, (TP, 1), 0)
    for g, w in enumerate(WINDOWS):
        cols = slice(g * GD, (g + 1) * GD)
        wsum = u[:, cols]
        for back in range(1, w):
            wsum = wsum + full_ref[halo - back:halo - back + TP, cols]
        count = jnp.minimum(pos + 1, w).astype(F32)
        diff = (wsum / count - u[:, cols]).astype(BF16)
        o_ref[:, cols] = (_dot(diff, w_ref[g]) * sc_ref[:, cols]).astype(BF16)
    full_ref[0:halo, :] = u[TP - halo:, :]


def _pool_prompt(u, w_pool, scale):
    ns = SEQ // TP
    return pl.pallas_call(
        _pool_prompt_kernel,
        grid=(BATCH, ns),
        in_specs=[pl.BlockSpec((TP, PW), lambda b, s: (b * ns + s, 0)),
                  pl.BlockSpec((len(WINDOWS), GD, GD), lambda b, s: (0, 0, 0)),
                  pl.BlockSpec((1, PW), lambda b, s: (0, 0))],
        out_specs=pl.BlockSpec((TP, PW), lambda b, s: (b * ns + s, 0)),
        out_shape=jax.ShapeDtypeStruct((MP, PW), BF16),
        scratch_shapes=[pltpu.VMEM((POOL_HIST + 1 + TP, PW), F32)],
        compiler_params=_cparams(("parallel", "arbitrary")),
        name="pool_prompt",
    )(u, w_pool, scale)


def _pool_sample_kernel(pos0, hist_ref, u_ref, w_ref, sc_ref, o_ref):
    def full_row(r, cols):
        if r < POOL_HIST:
            return hist_ref[:, r * PW + cols.start:r * PW + cols.stop]
        r -= POOL_HIST
        return u_ref[:, r * PW + cols.start:r * PW + cols.stop]

    for i in range(DEC):
        for g, w in enumerate(WINDOWS):
            cols = slice(g * GD, (g + 1) * GD)
            cur = full_row(POOL_HIST + i, cols)
            wsum = cur
            for back in range(1, w):
                wsum = wsum + full_row(POOL_HIST + i - back, cols)
            count = float(min(pos0 + i + 1, w))
            diff = (wsum / count - cur).astype(BF16)
            o_ref[i, :, cols] = (_dot(diff, w_ref[g]) * sc_ref[:, cols]).astype(BF16)


def _pool_sample(pos0, hist2, u2, w_pool, scale):
    return pl.pallas_call(
        functools.partial(_pool_sample_kernel, pos0),
        out_shape=jax.ShapeDtypeStruct((DEC, N_SEQ, PW), BF16),
        compiler_params=pltpu.CompilerParams(vmem_limit_bytes=VMEM_LIMIT),
        name="pool_sample",
    )(hist2, u2, w_pool, scale)


NP_TILES = MP // TM


def _prompt_tile(i):
    return jnp.minimum(i, NP_TILES - 1)


def _sample_tile(i):
    return jnp.maximum(i - NP_TILES, 0)


def _outproj_kernel(attp_ref, atts_ref, poolp_ref, pools_ref, wa_ref, wp_ref, xp_ref, xs_ref, o_ref):
    i = pl.program_id(0)

    def project(att_ref, pool_ref, x_ref):
        o_ref[...] = x_ref[...] + _dot(att_ref[...], wa_ref[...]) + _dot(pool_ref[...], wp_ref[...])

    @pl.when(i < NP_TILES)
    def _():
        project(attp_ref, poolp_ref, xp_ref)

    @pl.when(i >= NP_TILES)
    def _():
        project(atts_ref, pools_ref, xs_ref)


def _outproj(att_p, att_s, pooled_p, pooled_s, w, x_p, x_s):
    tn = D
    return pl.pallas_call(
        _outproj_kernel,
        grid=(M // TM, D // tn),
        in_specs=[pl.BlockSpec((TM, SBW), lambda i, j: (_prompt_tile(i), 0)),
                  pl.BlockSpec((TM, SBW), lambda i, j: (_sample_tile(i), 0)),
                  pl.BlockSpec((TM, PW), lambda i, j: (_prompt_tile(i), 0)),
                  pl.BlockSpec((TM, PW), lambda i, j: (_sample_tile(i), 0)),
                  pl.BlockSpec((SBW, tn), lambda i, j: (0, j)),
                  pl.BlockSpec((PW, tn), lambda i, j: (1, j)),
                  pl.BlockSpec((TM, tn), lambda i, j: (_prompt_tile(i), j)),
                  pl.BlockSpec((TM, tn), lambda i, j: (_sample_tile(i), j))],
        out_specs=pl.BlockSpec((TM, tn), lambda i, j: (i, j)),
        out_shape=jax.ShapeDtypeStruct((M, D), F32),
        compiler_params=_cparams(("parallel", "arbitrary")),
        name="outproj",
    )(att_p, att_s, pooled_p, pooled_s, w, w, x_p, x_s)


def _ffn_kernel(x_ref, g_ref, wg_ref, wu_ref, wd_ref, o_ref, xn_ref):
    f = pl.program_id(1)

    @pl.when(f == 0)
    def _():
        x = x_ref[...]
        xn_ref[...] = _rms_bf16(x, g_ref[...])
        o_ref[...] = x

    xn = xn_ref[...]
    gate = _dot(xn, wg_ref[...].astype(BF16))
    up = _dot(xn, wu_ref[...].astype(BF16))
    hid = (gate * jax.nn.sigmoid(gate) * up).astype(BF16)
    o_ref[...] += _dot(hid, wd_ref[...].astype(BF16))


def _ffn(x, g, wg, wu, wd):
    return pl.pallas_call(
        _ffn_kernel,
        grid=(M // TM_FFN, FF // TF),
        in_specs=[pl.BlockSpec((TM_FFN, D), lambda i, f: (i, 0), pipeline_mode=pl.Buffered(1)),
                  pl.BlockSpec((1, D), lambda i, f: (0, 0)),
                  pl.BlockSpec((D, TF), lambda i, f: (0, f)),
                  pl.BlockSpec((D, TF), lambda i, f: (0, f)),
                  pl.BlockSpec((TF, D), lambda i, f: (f, 0))],
        out_specs=pl.BlockSpec((TM_FFN, D), lambda i, f: (i, 0)),
        out_shape=jax.ShapeDtypeStruct((M, D), F32),
        scratch_shapes=[pltpu.VMEM((TM_FFN, D), BF16)],
        compiler_params=_cparams(("parallel", "arbitrary")),
        name="ffn",
    )(x, g, wg, wu, wd)


def _convin_kernel(x_ref, g_ref, wa_ref, wg_ref, o_ref, xn_ref):
    j = pl.program_id(1)

    @pl.when(j == 0)
    def _():
        xn_ref[...] = _rms_bf16(x_ref[...], g_ref[...])

    xn = xn_ref[...]
    a = _dot(xn, wa_ref[...])
    gate = _dot(xn, wg_ref[...])
    o_ref[...] = a * jax.nn.sigmoid(gate)


def _convin(x, g, w):
    tn = 1024
    nj = D // tn
    return pl.pallas_call(
        _convin_kernel,
        grid=(M // TM, nj),
        in_specs=[pl.BlockSpec((TM, D), lambda i, j: (i, 0)),
                  pl.BlockSpec((1, D), lambda i, j: (0, 0)),
                  pl.BlockSpec((D, tn), lambda i, j: (0, j)),
                  pl.BlockSpec((D, tn), lambda i, j: (0, nj + j))],
        out_specs=pl.BlockSpec((TM, tn), lambda i, j: (i, j)),
        out_shape=jax.ShapeDtypeStruct((M, D), F32),
        scratch_shapes=[pltpu.VMEM((TM, D), BF16)],
        compiler_params=_cparams(("parallel", "arbitrary")),
        name="convin",
    )(x, g, w, w)


def _ln_silu_bf16(h, lg, lb):
    mu = jnp.mean(h, axis=-1, keepdims=True)
    c = h - mu
    var = jnp.mean(c * c, axis=-1, keepdims=True)
    hn = c * lax.rsqrt(var + LN_EPS) * lg + lb
    return (hn * jax.nn.sigmoid(hn)).astype(BF16)


CONV_HALO = 32
CONV_RB = 64
CONV_CB = 256
CONV_LEN = CONV_HALO + TC - SUBLANES


def _conv_prompt_kernel(u_ref, dw_ref, db_ref, lg_ref, lb_ref, w_ref, x_ref, o_ref,
                        full_ref, shift_ref, hc_ref, hb_ref):
    s = pl.program_id(1)
    j = pl.program_id(2)

    @pl.when(j == 0)
    def _():
        @pl.when(s == 0)
        def _():
            full_ref[0:CONV_HALO, :] = jnp.zeros((CONV_HALO, D), F32)

        full_ref[CONV_HALO:CONV_HALO + TC, :] = u_ref[...]
        for b in range(1, SUBLANES):
            shift_ref[b - 1] = full_ref[b:b + CONV_LEN, :]
        first = CONV_HALO - CONV_HIST
        groups = [[a for a in range((first + CONV_W - 1) // SUBLANES + 1)
                   if 0 <= a * SUBLANES + b - first < CONV_W] for b in range(SUBLANES)]
        def row_block(rb, carry):
            base = rb * CONV_RB
            for cb in range(D // CONV_CB):
                cols = slice(cb * CONV_CB, (cb + 1) * CONV_CB)
                acc = jnp.zeros((CONV_RB, CONV_CB), F32)
                for b, tiles in enumerate(groups):
                    lo = pl.multiple_of(base + tiles[0] * SUBLANES, SUBLANES)
                    n = (tiles[-1] - tiles[0]) * SUBLANES + CONV_RB
                    window = (full_ref[pl.ds(lo, n), cols] if b == 0
                              else shift_ref[b - 1, pl.ds(lo, n), cols])
                    for a in tiles:
                        tap = a * SUBLANES + b - first
                        r = (a - tiles[0]) * SUBLANES
                        taps = jnp.concatenate([dw_ref[tap, :, cols]] * (CONV_RB // SUBLANES), axis=0)
                        acc = acc + window[r:r + CONV_RB] * taps
                hc_ref[pl.ds(pl.multiple_of(base, CONV_RB), CONV_RB), cols] = acc + db_ref[:, cols]
            return carry

        lax.fori_loop(0, TC // CONV_RB, row_block, 0)
        hb_ref[...] = _ln_silu_bf16(hc_ref[...], lg_ref[...], lb_ref[...])
        full_ref[0:CONV_HALO, :] = full_ref[TC:TC + CONV_HALO, :]

    o_ref[...] = x_ref[...] + _dot(hb_ref[...], w_ref[...])


def _conv_prompt(uc, dw, db, lg, lb, w, x):
    ns = SEQ // TC
    tn = 1024
    vec = pl.BlockSpec((1, D), lambda b, s, j: (0, 0))
    return pl.pallas_call(
        _conv_prompt_kernel,
        grid=(BATCH, ns, D // tn),
        in_specs=[pl.BlockSpec((TC, D), lambda b, s, j: (b * ns + s, 0)),
                  pl.BlockSpec((CONV_W, SUBLANES, D), lambda b, s, j: (0, 0, 0)),
                  vec, vec, vec,
                  pl.BlockSpec((D, tn), lambda b, s, j: (0, j)),
                  pl.BlockSpec((TC, tn), lambda b, s, j: (b * ns + s, j))],
        out_specs=pl.BlockSpec((TC, tn), lambda b, s, j: (b * ns + s, j)),
        out_shape=jax.ShapeDtypeStruct((MP, D), F32),
        scratch_shapes=[pltpu.VMEM((CONV_HALO + TC, D), F32),
                        pltpu.VMEM((SUBLANES - 1, CONV_LEN, D), F32),
                        pltpu.VMEM((TC, D), F32),
                        pltpu.VMEM((TC, D), BF16)],
        compiler_params=_cparams(("parallel", "arbitrary", "arbitrary")),
        name="conv_prompt",
    )(uc, dw, db, lg, lb, w, x)


CONV_NB = 32


def _conv_sample_kernel(hist_ref, u_ref, dw_ref, db_ref, lg_ref, lb_ref, w_ref, x_ref, o_ref):
    def full_row(r):
        if r < CONV_HIST:
            return hist_ref[:, r * D:(r + 1) * D]
        r -= CONV_HIST
        return u_ref[:, r * D:(r + 1) * D]

    for i in range(DEC):
        acc = jnp.zeros((CONV_NB, D), F32)
        for tap in range(CONV_W):
            acc = acc + full_row(i + tap) * dw_ref[tap:tap + 1, :]
        hb = _ln_silu_bf16(acc + db_ref[...], lg_ref[...], lb_ref[...])
        o_ref[i] = x_ref[i] + _dot(hb, w_ref[...])


def _conv_sample(hist2, u2, dw, db, lg, lb, w, x3):
    vec = pl.BlockSpec((1, D), lambda n: (0, 0))
    return pl.pallas_call(
        _conv_sample_kernel,
        grid=(N_SEQ // CONV_NB,),
        in_specs=[pl.BlockSpec((CONV_NB, CONV_HIST * D), lambda n: (n, 0)),
                  pl.BlockSpec((CONV_NB, DEC * D), lambda n: (n, 0)),
                  pl.BlockSpec((CONV_W, D), lambda n: (0, 0)),
                  vec, vec, vec,
                  pl.BlockSpec((D, D), lambda n: (0, 0)),
                  pl.BlockSpec((DEC, CONV_NB, D), lambda n: (0, n, 0))],
        out_specs=pl.BlockSpec((DEC, CONV_NB, D), lambda n: (0, n, 0)),
        out_shape=jax.ShapeDtypeStruct((DEC, N_SEQ, D), F32),
        compiler_params=_cparams(("parallel",)),
        name="conv_sample",
    )(hist2, u2, dw, db, lg, lb, w, x3)


NCH = SUBLANES
HALF = D // 2


def _store_packed(ref, base_row, rows, x):
    for c in range(NCH):
        lo = x[:, c * LANES:(c + 1) * LANES]
        hi = x[:, HALF + c * LANES:HALF + (c + 1) * LANES]
        ref[pl.ds(base_row * NCH + c, rows, stride=NCH), :] = _pack_pair(lo, hi)


def _load_packed(ref, base_row, rows, dst_ref, dtype):
    for c in range(NCH):
        lo, hi = _unpack_pair(ref[pl.ds(base_row * NCH + c, rows, stride=NCH), :])
        dst_ref[:, c * LANES:(c + 1) * LANES] = lo.astype(dtype)
        dst_ref[:, HALF + c * LANES:HALF + (c + 1) * LANES] = hi.astype(dtype)


def _router_kernel(xa_ref, xb_ref, g_ref, wr_ref, tri_ref, xp_ref, code_ref, gate_ref, cnt_ref,
                   carry_ref, x_ref):
    i = pl.program_id(0)

    @pl.when(i == 0)
    def _():
        carry_ref[...] = jnp.zeros_like(carry_ref)

    @pl.when(i < NP_TILES)
    def _():
        x_ref[...] = xa_ref[...]

    @pl.when(i >= NP_TILES)
    def _():
        x_ref[...] = xb_ref[...]

    x = x_ref[...]
    xn = x * lax.rsqrt(jnp.mean(x * x, axis=-1, keepdims=True) + RMS_EPS) * g_ref[...]
    _store_packed(xp_ref, 0, TM, xn)

    xh = xn.astype(BF16)
    xl = (xn - xh.astype(F32)).astype(BF16)
    w = wr_ref[...]
    wh = w.astype(BF16)
    wl = (w - wh.astype(F32)).astype(BF16)
    logits = _dot(jnp.concatenate([xh, xl, xh], axis=1), jnp.concatenate([wh, wh, wl], axis=0))
    lane = lax.broadcasted_iota(I32, (TM, N_EXP), 1)
    m1 = jnp.max(logits, axis=-1, keepdims=True)
    i1 = jnp.min(jnp.where(logits == m1, lane, N_EXP), axis=-1, keepdims=True)
    rest = jnp.where(lane == i1, -jnp.inf, logits)
    m2 = jnp.max(rest, axis=-1, keepdims=True)
    i2 = jnp.min(jnp.where(rest == m2, lane, N_EXP), axis=-1, keepdims=True)
    e = jnp.exp(m2 - m1)
    g1 = 1.0 / (1.0 + e)
    gate_ref[:, 0:1] = g1
    gate_ref[:, 1:2] = e * g1

    hot = jnp.where((lane == i1) | (lane == i2), 1.0, 0.0)
    before = _dot(tri_ref[...], hot.astype(BF16)) + carry_ref[...]
    r1 = jnp.sum(jnp.where(lane == i1, before, 0.0), axis=-1, keepdims=True)
    r2 = jnp.sum(jnp.where(lane == i2, before, 0.0), axis=-1, keepdims=True)
    code_ref[:, 0:1] = i1
    code_ref[:, 1:2] = i2
    code_ref[:, 2:3] = r1.astype(I32)
    code_ref[:, 3:4] = r2.astype(I32)
    carry_ref[...] += jnp.sum(hot, axis=0, keepdims=True)
    cnt_ref[...] = carry_ref[...].astype(I32)


def _router(x_p, x_s, g, wr, tri):
    return pl.pallas_call(
        _router_kernel,
        grid=(M // TM,),
        in_specs=[pl.BlockSpec((TM, D), lambda i: (_prompt_tile(i), 0)),
                  pl.BlockSpec((TM, D), lambda i: (_sample_tile(i), 0)),
                  pl.BlockSpec((1, D), lambda i: (0, 0)),
                  pl.BlockSpec((D, N_EXP), lambda i: (0, 0)),
                  pl.BlockSpec((TM, TM), lambda i: (0, 0))],
        out_specs=[pl.BlockSpec((TM * NCH, LANES), lambda i: (i, 0)),
                   pl.BlockSpec((TM, 4), lambda i: (i, 0)),
                   pl.BlockSpec((TM, 2), lambda i: (i, 0)),
                   pl.BlockSpec((1, N_EXP), lambda i: (0, 0))],
        out_shape=[jax.ShapeDtypeStruct((M * NCH, LANES), U32),
                   jax.ShapeDtypeStruct((M, 4), I32),
                   jax.ShapeDtypeStruct((M, 2), F32),
                   jax.ShapeDtypeStruct((1, N_EXP), I32)],
        scratch_shapes=[pltpu.VMEM((1, N_EXP), F32), pltpu.VMEM((TM, D), F32)],
        compiler_params=_cparams(("arbitrary",)),
        name="router",
    )(x_p, x_s, g, wr, tri)


def _dispatch_kernel(pos_ref, xp_ref, xs_ref, sem):
    i = pl.program_id(0)

    def tok(t, carry):
        for k in range(2):
            p = pos_ref[2 * (i * TM + t) + k]
            pltpu.make_async_copy(xp_ref.at[pl.ds(t * NCH, NCH), :],
                                  xs_ref.at[pl.ds(p * NCH, NCH), :], sem).start()
        return carry

    lax.fori_loop(0, TM, tok, 0)
    for _ in range(2):
        pltpu.make_async_copy(xp_ref, xs_ref.at[pl.ds(0, TM * NCH), :], sem).wait()


def _dispatch(pos, xp):
    grid_spec = pltpu.PrefetchScalarGridSpec(
        num_scalar_prefetch=1,
        grid=(M // TM,),
        in_specs=[pl.BlockSpec((TM * NCH, LANES), lambda i, pos: (i, 0))],
        out_specs=pl.BlockSpec(memory_space=pl.ANY),
        scratch_shapes=[pltpu.SemaphoreType.DMA(())],
    )
    return pl.pallas_call(
        _dispatch_kernel,
        grid_spec=grid_spec,
        out_shape=jax.ShapeDtypeStruct((NT_MOE * TM_MOE * NCH, LANES), U32),
        compiler_params=pltpu.CompilerParams(dimension_semantics=("arbitrary",), has_side_effects=True),
        name="dispatch",
    )(pos, xp)


def _moe_kernel(te_ref, tn_ref, nt_ref, xs_ref, wg_ref, wu_ref, wd_ref, ys_ref, x_ref, acc_ref):
    i = pl.program_id(0)
    f = pl.program_id(1)
    active = i < nt_ref[0]

    @pl.when(active & (f == 0))
    def _():
        _load_packed(xs_ref, 0, TM_MOE, x_ref, BF16)
        rows = lax.broadcasted_iota(I32, (TM_MOE, 1), 0)
        x_ref[...] = jnp.where(rows < tn_ref[i], x_ref[...], jnp.zeros((), BF16))
        acc_ref[...] = jnp.zeros_like(acc_ref)

    @pl.when(active)
    def _():
        x = x_ref[...]
        gate = _dot(x, wg_ref[...].astype(BF16))
        up = _dot(x, wu_ref[...].astype(BF16))
        hid = (gate * jax.nn.sigmoid(gate) * up).astype(BF16)
        acc_ref[...] += _dot(hid, wd_ref[...].astype(BF16))

    @pl.when(active & (f == pl.num_programs(1) - 1))
    def _():
        _store_packed(ys_ref, 0, TM_MOE, acc_ref[...])


def _moe(tile_expert, tile_rows, n_tiles, xs, wg, wu, wd):
    nf = FF // TF

    def tile(i, nt):
        return jnp.minimum(i, nt[0] - 1)

    def fidx(i, f, nt):
        return jnp.where(i < nt[0], f, nf - 1)

    grid_spec = pltpu.PrefetchScalarGridSpec(
        num_scalar_prefetch=3,
        grid=(NT_MOE, nf),
        in_specs=[pl.BlockSpec((TM_MOE * NCH, LANES), lambda i, f, te, tn, nt: (tile(i, nt), 0)),
                  pl.BlockSpec((None, D, TF), lambda i, f, te, tn, nt: (te[tile(i, nt)], 0, fidx(i, f, nt))),
                  pl.BlockSpec((None, D, TF), lambda i, f, te, tn, nt: (te[tile(i, nt)], 0, fidx(i, f, nt))),
                  pl.BlockSpec((None, TF, D), lambda i, f, te, tn, nt: (te[tile(i, nt)], fidx(i, f, nt), 0))],
        out_specs=pl.BlockSpec((TM_MOE * NCH, LANES), lambda i, f, te, tn, nt: (tile(i, nt), 0)),
        scratch_shapes=[pltpu.VMEM((TM_MOE, D), BF16), pltpu.VMEM((TM_MOE, D), F32)],
    )
    return pl.pallas_call(
        _moe_kernel,
        grid_spec=grid_spec,
        out_shape=jax.ShapeDtypeStruct((NT_MOE * TM_MOE * NCH, LANES), U32),
        compiler_params=_cparams(("arbitrary", "arbitrary")),
        name="moe",
    )(tile_expert, tile_rows, n_tiles, xs, wg, wu, wd)


def _combine_kernel(pos_ref, ys_ref, xa_ref, xb_ref, gate_ref, g_ref, yp_ref, ysm_ref,
                    buf_ref, y1_ref, y2_ref, sem):
    i = pl.program_id(0)
    n = pl.num_programs(0)
    rows = 2 * TM * NCH

    def issue(tile_idx, slot):
        def tok(t, carry):
            for k in range(2):
                p = pos_ref[2 * (tile_idx * TM + t) + k]
                pltpu.make_async_copy(ys_ref.at[pl.ds(p * NCH, NCH), :],
                                      buf_ref.at[slot, pl.ds((k * TM + t) * NCH, NCH), :],
                                      sem.at[slot]).start()
            return carry

        lax.fori_loop(0, TM, tok, 0)

    @pl.when(i == 0)
    def _():
        issue(0, 0)

    @pl.when(i + 1 < n)
    def _():
        issue(i + 1, (i + 1) % 2)

    slot = i % 2
    pltpu.make_async_copy(ys_ref.at[pl.ds(0, rows), :], buf_ref.at[slot], sem.at[slot]).wait()
    _load_packed(buf_ref.at[slot], 0, TM, y1_ref, F32)
    _load_packed(buf_ref.at[slot], TM, TM, y2_ref, F32)
    moe = gate_ref[:, 0:1] * y1_ref[...] + gate_ref[:, 1:2] * y2_ref[...]

    def finish(x_ref, o_ref):
        x = x_ref[...] + moe
        o_ref[...] = x * lax.rsqrt(jnp.mean(x * x, axis=-1, keepdims=True) + RMS_EPS) * g_ref[...]

    @pl.when(i < NP_TILES)
    def _():
        finish(xa_ref, yp_ref)

    @pl.when(i >= NP_TILES)
    def _():
        finish(xb_ref, ysm_ref)


def _combine(pos, ys, x_p, x_s, gates, g):
    grid_spec = pltpu.PrefetchScalarGridSpec(
        num_scalar_prefetch=1,
        grid=(M // TM,),
        in_specs=[pl.BlockSpec(memory_space=pl.ANY),
                  pl.BlockSpec((TM, D), lambda i, pos: (_prompt_tile(i), 0)),
                  pl.BlockSpec((TM, D), lambda i, pos: (_sample_tile(i), 0)),
                  pl.BlockSpec((TM, 2), lambda i, pos: (i, 0)),
                  pl.BlockSpec((1, D), lambda i, pos: (0, 0))],
        out_specs=[pl.BlockSpec((TM, D), lambda i, pos: (_prompt_tile(i), 0)),
                   pl.BlockSpec((TM, D), lambda i, pos: (_sample_tile(i), 0))],
        scratch_shapes=[pltpu.VMEM((2, 2 * TM * NCH, LANES), U32),
                        pltpu.VMEM((TM, D), F32),
                        pltpu.VMEM((TM, D), F32),
                        pltpu.SemaphoreType.DMA((2,))],
    )
    return pl.pallas_call(
        _combine_kernel,
        grid_spec=grid_spec,
        out_shape=[jax.ShapeDtypeStruct((MP, D), F32), jax.ShapeDtypeStruct((MS, D), F32)],
        compiler_params=_cparams(("arbitrary",)),
        name="combine",
    )(pos, ys, x_p, x_s, gates, g)


def _moe_schedule(counts, code):
    passes = jnp.maximum((counts + TM_MOE - 1) // TM_MOE, 1)
    rows_per_pass = (counts + passes - 1) // passes
    tile_end = jnp.cumsum(passes)
    tile_base = tile_end - passes
    n_tiles = tile_end[-1].astype(I32)
    experts = jnp.arange(N_EXP, dtype=I32)

    def per_expert(table, expert):
        return jnp.sum(jnp.where(expert[..., None] == experts, table, 0), axis=-1)

    tiles = jnp.arange(NT_MOE, dtype=I32)
    tile_expert = jnp.minimum(jnp.sum(tiles[:, None] >= tile_end[None, :], axis=1), N_EXP - 1).astype(I32)
    tile_pass = tiles - per_expert(tile_base, tile_expert)
    tile_rpp = per_expert(rows_per_pass, tile_expert)
    tile_rows = jnp.clip(per_expert(counts, tile_expert) - tile_pass * tile_rpp, 0, tile_rpp)
    tile_rows = jnp.where(tiles < n_tiles, tile_rows, 0).astype(I32)
    expert = code[:, 0:2]
    rank = code[:, 2:4]
    rpp = per_expert(rows_per_pass, expert)
    pas = sum((rank >= m * rpp).astype(I32) for m in range(1, pl.cdiv(M, TM_MOE)))
    pos = (per_expert(tile_base, expert) + pas) * TM_MOE + rank - pas * rpp
    return tile_expert, tile_rows, n_tiles.reshape(1), pos.reshape(-1).astype(I32)


def kernel(x_prompt, x_sample, cache_k, cache_v, state_pool, state_conv, page_table, norm_mix, norm_ffn, norm_out, w_in_ab, sb_bias, w_pool, pool_scale, w_out_ab, w_ffn_gate, w_ffn_up, w_ffn_down, w_in_conv, conv_dw, conv_dw_bias, conv_ln_gain, conv_ln_bias, w_out_conv, w_router, w_exp_gate, w_exp_up, w_exp_down):
    xp = x_prompt.reshape(MP, D)
    xs = x_sample.reshape(MS, D)
    past_len = page_table.shape[1] * PAGE

    w_in = w_in_ab[0].astype(BF16)
    g0 = norm_mix[0:1]
    q_p, k_p, v_p, u_p, kb_p, vb_p = _inproj(xp, g0, w_in)
    q_s, k_s, v_s, u_s, kb_s, vb_s = _inproj(xs, g0, w_in)

    tri_q = jnp.tile(jnp.tril(jnp.ones((TQ, TQ), BF16)), (2, 1))
    att_p = _attn_prompt(sb_bias[0], q_p, kb_p, vb_p, tri_q)

    def heads_major(t):
        t = t.reshape(N_SEQ, DEC, HEADS, DH).transpose(0, 2, 1, 3)
        return jnp.pad(t, ((0, 0), (0, 0), (0, QPAD - DEC), (0, 0)))

    tri_p = jnp.tile(jnp.tril(jnp.ones((PAGE, PAGE), BF16)), (2, 1))
    bias_rows = jnp.broadcast_to(jnp.repeat(sb_bias[0], QPAD)[:, None], (HEADS * QPAD, PAGE)).astype(F32)
    ck2 = cache_k[0].reshape(-1, DH)
    cv2 = cache_v[0].reshape(-1, DH)
    o_s = _attn_sample(page_table, heads_major(q_s), heads_major(kb_s), heads_major(vb_s),
                       bias_rows, tri_p, ck2, cv2)
    att_s = o_s[:, :, :DEC].transpose(0, 2, 1, 3).reshape(MS, SBW).astype(BF16)

    w_pl = w_pool[0].astype(BF16)
    psc = pool_scale[0:1]
    pooled_p = _pool_prompt(u_p, w_pl, psc)
    pooled_s = _pool_sample(past_len, state_pool[0].reshape(N_SEQ, POOL_HIST * PW),
                            u_s.reshape(N_SEQ, DEC * PW), w_pl, psc)
    pooled_s = pooled_s.transpose(1, 0, 2).reshape(MS, PW)

    w_out = w_out_ab[0].astype(BF16)
    x1 = _outproj(att_p, att_s, pooled_p, pooled_s, w_out, xp, xs)

    x1 = _ffn(x1, norm_ffn[0:1], w_ffn_gate[0], w_ffn_up[0], w_ffn_down[0])

    uc = _convin(x1, norm_mix[1:2], w_in_conv[0].astype(BF16))
    w_oc = w_out_conv[0].astype(BF16)
    dwb = conv_dw_bias[0:1]
    lg = conv_ln_gain[0:1]
    lb = conv_ln_bias[0:1]
    dw_tiles = jnp.broadcast_to(conv_dw[0][:, None, :], (CONV_W, SUBLANES, D))
    x2_p = _conv_prompt(uc, dw_tiles, dwb, lg, lb, w_oc, x1)
    uc_s = uc[MP:]
    x1_s3 = x1[MP:].reshape(N_SEQ, DEC, D).transpose(1, 0, 2)
    x2_s3 = _conv_sample(state_conv[0].reshape(N_SEQ, CONV_HIST * D), uc_s.reshape(N_SEQ, DEC * D),
                         conv_dw[0], dwb, lg, lb, w_oc, x1_s3)
    x2_s = x2_s3.transpose(1, 0, 2).reshape(MS, D)

    tri_r = jnp.tril(jnp.ones((TM, TM), BF16), -1)
    xpk, code, gates, counts = _router(x2_p, x2_s, norm_ffn[1:2], w_router[0], tri_r)
    tile_expert, tile_rows, n_tiles, pos = _moe_schedule(counts[0], code)
    xsorted = _dispatch(pos, xpk)
    ysorted = _moe(tile_expert, tile_rows, n_tiles, xsorted, w_exp_gate[0], w_exp_up[0], w_exp_down[0])
    y_p, y_s = _combine(pos, ysorted, x2_p, x2_s, gates, norm_out.reshape(1, D))

    prompt_pool = jnp.stack([u_p[(b + 1) * SEQ - POOL_HIST:(b + 1) * SEQ] for b in range(BATCH)])
    prompt_conv = jnp.stack([uc[(b + 1) * SEQ - CONV_HIST:(b + 1) * SEQ] for b in range(BATCH)])
    sample_pool = jnp.concatenate([state_pool[0][:, DEC:], u_s.reshape(N_SEQ, DEC, PW)], axis=1)
    sample_conv = jnp.concatenate([state_conv[0][:, DEC:], uc_s.reshape(N_SEQ, DEC, D)], axis=1)
    return (y_p.reshape(BATCH, SEQ, D),
            y_s.reshape(N_SEQ, DEC, D),
            k_p.reshape(1, BATCH, SEQ, HEADS, DH),
            v_p.reshape(1, BATCH, SEQ, HEADS, DH),
            prompt_pool[None],
            prompt_conv[None],
            k_s.reshape(1, N_SEQ, DEC, HEADS, DH),
            v_s.reshape(1, N_SEQ, DEC, HEADS, DH),
            sample_pool[None],
            sample_conv[None])
```
